```python
import jax, jax.numpy as jnp
from jax import lax
import numpy as np

D_MODEL = 1024
BATCH = 1
SEQ = 16384
DEPTH = 4

CHUNK = 64
HEAD_DIM = 64
N_HEADS_SB = 4
N_HEADS_CH = 8
CONV_CH = 256
CONV_WIDTH = 3
N_PREV_CHUNKS = 8
BAND = (N_PREV_CHUNKS + 1) * CHUNK
REL_CLIP = 128
Q_BLOCK = 128
D_SB = N_HEADS_SB * HEAD_DIM
D_CH = N_HEADS_CH * HEAD_DIM
D_MIX = D_SB + D_CH + CONV_CH
D_IN = 3 * D_SB + 3 * D_CH + 3 * CONV_CH
N_OUT_GROUPS = D_MIX // HEAD_DIM
D_FF = ((8 * D_MODEL // 3 + 255) // 256) * 256
EPS = 1e-6

kernel_name = "hybrid_stickbreak_chunkattn_shortconv_block"


def rmsnorm(x, w):
    xf = x.astype(jnp.float32)
    y = xf * lax.rsqrt(jnp.mean(xf * xf, axis=-1, keepdims=True) + EPS)
    return (y * w.astype(jnp.float32)).astype(x.dtype)


def group_rmsnorm(y, w):
    b, s, _ = y.shape
    yg = y.astype(jnp.float32).reshape(b, s, N_OUT_GROUPS, HEAD_DIM)
    yg = yg * lax.rsqrt(jnp.mean(yg * yg, axis=-1, keepdims=True) + EPS)
    return (yg.reshape(b, s, D_MIX) * w.astype(jnp.float32)).astype(y.dtype)


def to_heads(t):
    b, s, _ = t.shape
    return t.reshape(b, s, -1, HEAD_DIM).transpose(0, 2, 1, 3)


def from_heads(t):
    b, h, s, d = t.shape
    return t.transpose(0, 2, 1, 3).reshape(b, s, h * d)


def stick_breaking_attention(q, k, v):
    b, h, s, d = q.shape
    n = s // Q_BLOCK
    qb = (q.astype(jnp.float32) * (d ** -0.5)).reshape(b, h, n, Q_BLOCK, d)
    kb = k.astype(jnp.float32).reshape(b, h, n, Q_BLOCK, d)
    vb = v.astype(jnp.float32).reshape(b, h, n, Q_BLOCK, d)
    idx = jnp.arange(Q_BLOCK)
    after_mat = (idx[:, None] > idx[None, :]).astype(jnp.float32)
    diag_mask = idx[None, :] < idx[:, None]
    out = jnp.zeros((b, h, n, Q_BLOCK, d), jnp.float32)
    carry = jnp.zeros((b, h, n, Q_BLOCK), jnp.float32)
    for o in range(n):
        z = jnp.einsum('bhnqd,bhnkd->bhnqk', qb[:, :, o:], kb[:, :, :n - o])
        log_rest = jax.nn.log_sigmoid(-z)
        if o == 0:
            log_rest = jnp.where(diag_mask, log_rest, 0.0)
        after = jnp.einsum('bhnqj,js->bhnqs', log_rest, after_mat) + carry[:, :, o:, :, None]
        w = jnp.exp(jax.nn.log_sigmoid(z) + after)
        if o == 0:
            w = jnp.where(diag_mask, w, 0.0)
        out = out.at[:, :, o:].add(jnp.einsum('bhnqk,bhnkd->bhnqd', w, vb[:, :, :n - o]))
        carry = carry.at[:, :, o:].add(jnp.sum(log_rest, axis=-1))
    return out.reshape(b, h, s, d).astype(v.dtype)


def chunked_relpos_attention(q, k, v, rel_bias):
    b, h, s, d = q.shape
    nc = s // CHUNK
    pad = N_PREV_CHUNKS * CHUNK
    qc = (q.astype(jnp.float32) * (d ** -0.5)).reshape(b, h, nc, CHUNK, d)
    kpad = jnp.pad(k, ((0, 0), (0, 0), (pad, 0), (0, 0))).reshape(b, h, nc + N_PREV_CHUNKS, CHUNK, d)
    vpad = jnp.pad(v, ((0, 0), (0, 0), (pad, 0), (0, 0))).reshape(b, h, nc + N_PREV_CHUNKS, CHUNK, d)
    band_idx = jnp.arange(nc)[:, None] + jnp.arange(N_PREV_CHUNKS + 1)[None, :]
    kb = kpad[:, :, band_idx].reshape(b, h, nc, BAND, d).astype(jnp.float32)
    vb = vpad[:, :, band_idx].reshape(b, h, nc, BAND, d).astype(jnp.float32)
    scores = jnp.einsum('bhcqd,bhckd->bhcqk', qc, kb)
    p = jnp.arange(CHUNK)[:, None]
    m = jnp.arange(BAND)[None, :]
    rel = jnp.clip(N_PREV_CHUNKS * CHUNK + p - m, -REL_CLIP, REL_CLIP) + REL_CLIP
    bias = rel_bias.astype(jnp.float32)[:, rel]
    valid = jnp.repeat(band_idx >= N_PREV_CHUNKS, CHUNK, axis=1)
    scores = jnp.where(valid[None, None, :, None, :], scores + bias[None, :, None], -jnp.inf)
    probs = jax.nn.softmax(scores, axis=-1)
    out = jnp.einsum('bhcqk,bhckd->bhcqd', probs, vb)
    return out.reshape(b, h, s, d).astype(v.dtype)


def short_conv_mixer(gate_b, gate_c, xc, conv_w):
    hc = gate_c * xc
    y = lax.conv_general_dilated(
        hc, conv_w[:, None, :].astype(hc.dtype), window_strides=(1,),
        padding=[(CONV_WIDTH - 1, 0)], dimension_numbers=('NWC', 'WIO', 'NWC'),
        feature_group_count=CONV_CH)
    return gate_b * y


def swiglu(x, w_gate, w_up, w_down):
    return (jax.nn.silu(x @ w_gate) * (x @ w_up)) @ w_down


def setup_inputs(seed: int = 0) -> dict:
    key = jax.random.key(seed)
    ks = jax.random.split(key, 14)
    f32 = jnp.float32

    def gain(k, n):
        return 1.0 + 0.02 * jax.random.normal(k, (DEPTH, n), f32)

    return {
        "x": jax.random.normal(ks[0], (BATCH, SEQ, D_MODEL), f32),
        "attn_norm_w": gain(ks[1], D_MODEL),
        "w_in": jax.random.normal(ks[2], (DEPTH, D_MODEL, D_IN), f32) * D_MODEL ** -0.5,
        "q_norm_w": gain(ks[3], HEAD_DIM),
        "k_norm_w": gain(ks[4], HEAD_DIM),
        "rel_bias": 0.1 * jax.random.normal(ks[5], (DEPTH, N_HEADS_CH, 2 * REL_CLIP + 1), f32),
        "conv_w": jax.random.normal(ks[6], (DEPTH, CONV_WIDTH, CONV_CH), f32) * CONV_WIDTH ** -0.5,
        "out_norm_w": gain(ks[7], D_MIX),
        "w_out": jax.random.normal(ks[8], (DEPTH, D_MIX, D_MODEL), f32) * D_MIX ** -0.5,
        "ffn_norm_w": gain(ks[9], D_MODEL),
        "w_gate": jax.random.normal(ks[10], (DEPTH, D_MODEL, D_FF), f32) * D_MODEL ** -0.5,
        "w_up": jax.random.normal(ks[11], (DEPTH, D_MODEL, D_FF), f32) * D_MODEL ** -0.5,
        "w_down": jax.random.normal(ks[12], (DEPTH, D_FF, D_MODEL), f32) * D_FF ** -0.5,
    }


def reference(x, attn_norm_w, w_in, q_norm_w, k_norm_w, rel_bias, conv_w, out_norm_w,
              w_out, ffn_norm_w, w_gate, w_up, w_down):
    widths = [D_SB] * 3 + [D_CH] * 3 + [CONV_CH] * 3
    split_points = [int(v) for v in np.cumsum(widths)[:-1]]
    for l in range(DEPTH):
        h = rmsnorm(x, attn_norm_w[l])
        proj = h @ w_in[l]
        (q_a, k_a, v_a, q_b, k_b, v_b, g_b, g_c, x_c) = jnp.split(proj, split_points, axis=-1)
        y_sb = from_heads(stick_breaking_attention(to_heads(q_a), to_heads(k_a), to_heads(v_a)))
        qh = rmsnorm(to_heads(q_b), q_norm_w[l])
        kh = rmsnorm(to_heads(k_b), k_norm_w[l])
        y_ch = from_heads(chunked_relpos_attention(qh, kh, to_heads(v_b), rel_bias[l]))
        y_cv = short_conv_mixer(g_b, g_c, x_c, conv_w[l])
        y = group_rmsnorm(jnp.concatenate([y_sb, y_ch, y_cv], axis=-1), out_norm_w[l])
        x = x + y @ w_out[l]
        x = x + swiglu(rmsnorm(x, ffn_norm_w[l]), w_gate[l], w_up[l], w_down[l])
    return x
```

```python
import functools
import math

import jax
import jax.numpy as jnp
import numpy as np
from jax import lax
from jax.experimental import pallas as pl
from jax.experimental.pallas import tpu as pltpu

F32 = jnp.float32
BF16 = jnp.bfloat16

D_MODEL = 1024
HEAD_DIM = 64
N_HEADS_SB = 4
N_HEADS_CH = 8
CONV_CH = 256
CONV_WIDTH = 3
CHUNK = 64
N_PREV_CHUNKS = 8
BAND = (N_PREV_CHUNKS + 1) * CHUNK
REL_CLIP = 128
D_SB = N_HEADS_SB * HEAD_DIM
D_CH = N_HEADS_CH * HEAD_DIM
D_IN = 3 * D_SB + 3 * D_CH + 3 * CONV_CH
D_FF = 2816
EPS = 1e-6

LOG2E = 1.4426950408889634
QK_SCALE = HEAD_DIM ** -0.5
NEG_BIG = -1e30

VMEM_LIMIT_BYTES = 56 * 1024 * 1024

TM_IN = 512
TM_OUT = 512
SB_T = 256
CH_TQ = 512
CH_SUB = 128
CH_WIN = CH_SUB + N_PREV_CHUNKS * CHUNK


def _dot(a, b):
    return jnp.dot(a, b, preferred_element_type=F32)


def _dot_nt(a, b):
    return lax.dot_general(a, b, (((1,), (1,)), ((), ())), preferred_element_type=F32)


def _group_mean_sq(x, g):
    w = g.shape[0]
    sq = x * x
    hi = sq.astype(BF16)
    lo = (sq - hi.astype(F32)).astype(BF16)
    parts = []
    for c in range(0, x.shape[1], w):
        parts.append(_dot(hi[:, c:c + w], g) + _dot(lo[:, c:c + w], g))
    return parts[0] if len(parts) == 1 else jnp.concatenate(parts, axis=1)


def _in_proj_kernel(x_ref, nw_ref, w_ref, wqt_ref, wvt_ref, qnw_ref, knw_ref, cw_ref, onw_ref, g_ref,
                    qat_ref, ka_ref, vat_ref, qb_ref, kb_ref, vb_ref, ycv_ref, hbuf_ref):
    i = pl.program_id(0)
    tm = x_ref.shape[0]
    xf = x_ref[...]
    ms = jnp.mean(xf * xf, axis=-1, keepdims=True)
    h = ((xf * lax.rsqrt(ms + EPS)) * nw_ref[...]).astype(BF16)

    qat = _dot_nt(wqt_ref[...], h) * (QK_SCALE * LOG2E)
    qat_ref[...] = qat.astype(BF16)
    vat = _dot_nt(wvt_ref[...], h).astype(BF16)
    for t in range(tm // SB_T):
        vat_ref[t] = vat[:, t * SB_T:(t + 1) * SB_T]
    ka = _dot(h, w_ref[:, D_SB:2 * D_SB]).astype(BF16)
    ka_ref[...] = ka.reshape(tm // SB_T, SB_T, D_SB)

    rest = _dot(h, w_ref[:, 3 * D_SB:])
    g = g_ref[...]
    q_b = rest[:, 0:D_CH]
    k_b = rest[:, D_CH:2 * D_CH]
    qn = (q_b * lax.rsqrt(_group_mean_sq(q_b, g) + EPS)) * qnw_ref[...]
    qb_ref[...] = (qn * (QK_SCALE * LOG2E)).astype(BF16)
    kn = (k_b * lax.rsqrt(_group_mean_sq(k_b, g) + EPS)) * knw_ref[...]
    kb_ref[...] = kn.astype(BF16)
    vb_ref[...] = rest[:, 2 * D_CH:3 * D_CH].astype(BF16)

    off = 3 * D_CH
    g_b = rest[:, off:off + CONV_CH]
    hc = rest[:, off + CONV_CH:off + 2 * CONV_CH] * rest[:, off + 2 * CONV_CH:off + 3 * CONV_CH]

    @pl.when(i == 0)
    def _():
        hbuf_ref[0:8, :] = jnp.zeros((8, CONV_CH), F32)

    hbuf_ref[8:8 + tm, :] = hc
    h1 = hbuf_ref[7:7 + tm, :]
    h2 = hbuf_ref[6:6 + tm, :]
    cw = cw_ref[...]
    ycv = g_b * (cw[0:1, :] * h2 + cw[1:2, :] * h1 + cw[2:3, :] * hc)
    hbuf_ref[0:8, :] = hbuf_ref[tm:tm + 8, :]
    yn = (ycv * lax.rsqrt(_group_mean_sq(ycv, g) + EPS)) * onw_ref[...]
    ycv_ref[...] = yn.astype(BF16)


def _in_proj(x, nw, w, wqt, wvt, qnw, knw, cw, onw_cv, g256):
    s = x.shape[0]
    tm = TM_IN
    nk = s // SB_T
    const = lambda shape: pl.BlockSpec(shape, lambda i: (0,) * len(shape), pipeline_mode=pl.Buffered(1))
    return pl.pallas_call(
        _in_proj_kernel,
        grid=(s // tm,),
        in_specs=[
            pl.BlockSpec((tm, D_MODEL), lambda i: (i, 0)),
            const((1, D_MODEL)),
            const((D_MODEL, D_IN)),
            const((D_SB, D_MODEL)),
            const((D_SB, D_MODEL)),
            const((1, D_CH)),
            const((1, D_CH)),
            const((CONV_WIDTH, CONV_CH)),
            const((1, CONV_CH)),
            const((256, 256)),
        ],
        out_specs=[
            pl.BlockSpec((D_SB, tm), lambda i: (0, i)),
            pl.BlockSpec((tm // SB_T, SB_T, D_SB), lambda i: (i, 0, 0)),
            pl.BlockSpec((tm // SB_T, D_SB, SB_T), lambda i: (i, 0, 0)),
            pl.BlockSpec((tm, D_CH), lambda i: (i, 0)),
            pl.BlockSpec((tm, D_CH), lambda i: (i, 0)),
            pl.BlockSpec((tm, D_CH), lambda i: (i, 0)),
            pl.BlockSpec((tm, CONV_CH), lambda i: (i, 0)),
        ],
        out_shape=[
            jax.ShapeDtypeStruct((D_SB, s), BF16),
            jax.ShapeDtypeStruct((nk, SB_T, D_SB), BF16),
            jax.ShapeDtypeStruct((nk, D_SB, SB_T), BF16),
            jax.ShapeDtypeStruct((s, D_CH), BF16),
            jax.ShapeDtypeStruct((s, D_CH), BF16),
            jax.ShapeDtypeStruct((s, D_CH), BF16),
            jax.ShapeDtypeStruct((s, CONV_CH), BF16),
        ],
        scratch_shapes=[pltpu.VMEM((tm + 8, CONV_CH), F32)],
        compiler_params=pltpu.CompilerParams(
            dimension_semantics=("arbitrary",), vmem_limit_bytes=VMEM_LIMIT_BYTES),
        name="in_proj",
    )(x, nw, w, wqt, wvt, qnw, knw, cw, onw_cv, g256)


def _sb_tile(j, diag, qm_ref, ka_ref, vat_ref, tri, acc_ref, carry_ref):
    kt = ka_ref[j]
    vt = vat_ref[j]
    if diag:
        s_idx = lax.broadcasted_iota(jnp.int32, (SB_T, SB_T), 0)
        t_idx = lax.broadcasted_iota(jnp.int32, (SB_T, SB_T), 1)
        mask = s_idx < t_idx
    for h in range(N_HEADS_SB):
        z = _dot(kt, qm_ref[:, h * SB_T:(h + 1) * SB_T])
        e = jnp.exp2(-jnp.abs(z))
        lr = jnp.minimum(-z, 0.0) - jnp.log2(1.0 + e)
        if diag:
            lr = jnp.where(mask, lr, 0.0)
        incl = _dot(tri, lr.astype(BF16))
        c = carry_ref[h]
        w = jnp.exp2(z + incl + c)
        if diag:
            w = jnp.where(mask, w, 0.0)
        carry_ref[h] = c + incl[0:1, :]
        acc_ref[h * HEAD_DIM:(h + 1) * HEAD_DIM, :] += _dot(
            vt[h * HEAD_DIM:(h + 1) * HEAD_DIM, :], w.astype(BF16))


def _sb_kernel(qat_ref, ka_ref, vat_ref, tri_ref, onw_ref, o_ref, qm_ref, acc_ref, carry_ref):
    i = pl.program_id(0)
    qt = qat_ref[...]
    row_head = lax.broadcasted_iota(jnp.int32, qt.shape, 0) // HEAD_DIM
    for h in range(N_HEADS_SB):
        qm_ref[:, h * SB_T:(h + 1) * SB_T] = jnp.where(row_head == h, qt, jnp.zeros_like(qt))
    acc_ref[...] = jnp.zeros_like(acc_ref)
    carry_ref[...] = jnp.zeros_like(carry_ref)
    tri = tri_ref[...]

    _sb_tile(i, True, qm_ref, ka_ref, vat_ref, tri, acc_ref, carry_ref)

    def body(jj, carry):
        _sb_tile(i - jj, False, qm_ref, ka_ref, vat_ref, tri, acc_ref, carry_ref)
        return carry

    lax.fori_loop(1, i + 1, body, 0)

    acc = acc_ref[...]
    parts = []
    for h in range(N_HEADS_SB):
        a = acc[h * HEAD_DIM:(h + 1) * HEAD_DIM, :]
        ms = jnp.mean(a * a, axis=0, keepdims=True)
        parts.append(a * lax.rsqrt(ms + EPS))
    y = jnp.concatenate(parts, axis=0) * onw_ref[...]
    o_ref[...] = y.T.astype(BF16)


def _stickbreak(qat, ka3, vat3, tri, onw_b):
    nk = ka3.shape[0]
    s = nk * SB_T
    const = lambda shape: pl.BlockSpec(shape, lambda i: (0,) * len(shape), pipeline_mode=pl.Buffered(1))
    return pl.pallas_call(
        _sb_kernel,
        grid=(nk,),
        in_specs=[
            pl.BlockSpec((D_SB, SB_T), lambda i: (0, i)),
            const((nk, SB_T, D_SB)),
            const((nk, D_SB, SB_T)),
            const((SB_T, SB_T)),
            const((D_SB, SB_T)),
        ],
        out_specs=pl.BlockSpec((SB_T, D_SB), lambda i: (i, 0)),
        out_shape=jax.ShapeDtypeStruct((s, D_SB), BF16),
        scratch_shapes=[
            pltpu.VMEM((D_SB, N_HEADS_SB * SB_T), BF16),
            pltpu.VMEM((D_SB, SB_T), F32),
            pltpu.VMEM((N_HEADS_SB, 1, SB_T), F32),
        ],
        compiler_params=pltpu.CompilerParams(
            dimension_semantics=("arbitrary",), vmem_limit_bytes=VMEM_LIMIT_BYTES),
        name="stickbreak",
    )(qat, ka3, vat3, tri, onw_b)


def _chunk_kernel(q_ref, kp_ref, kc_ref, vp_ref, vc_ref, bias_ref, onw_ref, g_ref, o_ref, kbuf_ref, vbuf_ref):
    i = pl.program_id(0)
    tq = q_ref.shape[0]
    kbuf_ref[0:tq, :] = kp_ref[...]
    kbuf_ref[tq:2 * tq, :] = kc_ref[...]
    vbuf_ref[0:tq, :] = vp_ref[...]
    vbuf_ref[tq:2 * tq, :] = vc_ref[...]
    col = lax.broadcasted_iota(jnp.int32, (1, 2 * tq), 1)
    kmask = jnp.where(jnp.logical_and(col < tq, i == 0), NEG_BIG, 0.0).astype(F32)
    lane = lax.broadcasted_iota(jnp.int32, (CH_SUB, 2 * HEAD_DIM), 1)
    low = lane < HEAD_DIM
    g = g_ref[...]
    for sub in range(tq // CH_SUB):
        r0 = sub * CH_SUB
        km = kmask[:, r0:r0 + CH_WIN]
        for p in range(N_HEADS_CH // 2):
            c0 = p * 2 * HEAD_DIM
            qp = q_ref[r0:r0 + CH_SUB, c0:c0 + 2 * HEAD_DIM]
            kw = kbuf_ref[r0:r0 + CH_WIN, c0:c0 + 2 * HEAD_DIM]
            vw = vbuf_ref[r0:r0 + CH_WIN, c0:c0 + 2 * HEAD_DIM]
            outs = []
            for hh in range(2):
                qm = jnp.where(low if hh == 0 else jnp.logical_not(low), qp, jnp.zeros_like(qp))
                sc = _dot_nt(qm, kw) + bias_ref[2 * p + hh] + km
                m = jnp.max(sc, axis=-1, keepdims=True)
                pr = jnp.exp2(sc - m)
                l = jnp.sum(pr, axis=-1, keepdims=True)
                outs.append(_dot(pr.astype(BF16), vw) / l)
            o = jnp.where(low, outs[0], outs[1])
            y = (o * lax.rsqrt(_group_mean_sq(o, g) + EPS)) * onw_ref[:, c0:c0 + 2 * HEAD_DIM]
            o_ref[r0:r0 + CH_SUB, c0:c0 + 2 * HEAD_DIM] = y.astype(BF16)


def _chunkattn(qb, kb, vb, bias_tab, onw_ch, g128):
    s = qb.shape[0]
    tq = CH_TQ
    const = lambda shape: pl.BlockSpec(shape, lambda i: (0,) * len(shape), pipeline_mode=pl.Buffered(1))
    cur = pl.BlockSpec((tq, D_CH), lambda i: (i, 0))
    prev = pl.BlockSpec((tq, D_CH), lambda i: (jnp.maximum(i - 1, 0), 0))
    return pl.pallas_call(
        _chunk_kernel,
        grid=(s // tq,),
        in_specs=[cur, prev, cur, prev, cur,
                  const((N_HEADS_CH, CH_SUB, CH_WIN)), const((1, D_CH)), const((128, 128))],
        out_specs=pl.BlockSpec((tq, D_CH), lambda i: (i, 0)),
        out_shape=jax.ShapeDtypeStruct((s, D_CH), BF16),
        scratch_shapes=[pltpu.VMEM((2 * tq, D_CH), BF16), pltpu.VMEM((2 * tq, D_CH), BF16)],
        compiler_params=pltpu.CompilerParams(
            dimension_semantics=("arbitrary",), vmem_limit_bytes=VMEM_LIMIT_BYTES),
        name="chunkattn",
    )(qb, kb, kb, vb, vb, bias_tab, onw_ch, g128)


def _out_ffn_kernel(x_ref, ysb_ref, ych_ref, ycv_ref, wo_ref, fnw_ref, wg_ref, wu_ref, wd_ref, o_ref):
    x1 = (x_ref[...]
          + _dot(ysb_ref[...], wo_ref[0:D_SB, :])
          + _dot(ych_ref[...], wo_ref[D_SB:D_SB + D_CH, :])
          + _dot(ycv_ref[...], wo_ref[D_SB + D_CH:, :]))
    ms = jnp.mean(x1 * x1, axis=-1, keepdims=True)
    h = ((x1 * lax.rsqrt(ms + EPS)) * fnw_ref[...]).astype(BF16)
    gate = _dot(h, wg_ref[...])
    up = _dot(h, wu_ref[...])
    act = ((gate * (1.0 / (1.0 + jnp.exp(-gate)))) * up).astype(BF16)
    o_ref[...] = x1 + _dot(act, wd_ref[...])


def _out_ffn(x, ysb, ych, ycv, wo, fnw, wg, wu, wd):
    s = x.shape[0]
    tm = TM_OUT
    const = lambda shape: pl.BlockSpec(shape, lambda i: (0,) * len(shape), pipeline_mode=pl.Buffered(1))
    row = lambda w: pl.BlockSpec((tm, w), lambda i: (i, 0))
    return pl.pallas_call(
        _out_ffn_kernel,
        grid=(s // tm,),
        in_specs=[row(D_MODEL), row(D_SB), row(D_CH), row(CONV_CH),
                  const((D_MODEL, D_MODEL)), const((1, D_MODEL)),
                  const((D_MODEL, D_FF)), const((D_MODEL, D_FF)), const((D_FF, D_MODEL))],
        out_specs=row(D_MODEL),
        out_shape=jax.ShapeDtypeStruct((s, D_MODEL), F32),
        compiler_params=pltpu.CompilerParams(
            dimension_semantics=("arbitrary",), vmem_limit_bytes=VMEM_LIMIT_BYTES),
        name="out_ffn",
    )(x, ysb, ych, ycv, wo, fnw, wg, wu, wd)


def _block_diag_mean(width):
    idx = np.arange(width) // HEAD_DIM
    return jnp.asarray((idx[:, None] == idx[None, :]).astype(np.float32) / HEAD_DIM, dtype=BF16)


def _bias_table(rel_bias_l):
    p = np.arange(CHUNK)[:, None]
    m = np.arange(BAND)[None, :]
    rel = np.clip(N_PREV_CHUNKS * CHUNK + p - m, -REL_CLIP, REL_CLIP) + REL_CLIP
    bias = rel_bias_l.astype(F32)[:, rel] * LOG2E
    tab = jnp.full((N_HEADS_CH, CH_SUB, CH_WIN), NEG_BIG, F32)
    tab = tab.at[:, 0:CHUNK, 0:BAND].set(bias)
    tab = tab.at[:, CHUNK:2 * CHUNK, CHUNK:CHUNK + BAND].set(bias)
    return tab


def kernel(x, attn_norm_w, w_in, q_norm_w, k_norm_w, rel_bias, conv_w, out_norm_w, w_out, ffn_norm_w,
           w_gate, w_up, w_down):
    b, s, d = x.shape
    assert b == 1 and d == D_MODEL and s % TM_IN == 0 and s % TM_OUT == 0 and s % CH_TQ == 0
    depth = w_in.shape[0]
    xs = x.reshape(s, d)
    g256 = _block_diag_mean(256)
    g128 = _block_diag_mean(128)
    ar = np.arange(SB_T)
    tri = jnp.asarray((ar[None, :] >= ar[:, None]).astype(np.float32), dtype=BF16)
    for l in range(depth):
        w_l = w_in[l].astype(BF16)
        wqt = w_in[l][:, 0:D_SB].T.astype(BF16)
        wvt = w_in[l][:, 2 * D_SB:3 * D_SB].T.astype(BF16)
        onw = out_norm_w[l].astype(F32)
        qat, ka3, vat3, qb, kb, vb, ycv = _in_proj(
            xs, attn_norm_w[l].reshape(1, d), w_l, wqt, wvt,
            jnp.tile(q_norm_w[l], N_HEADS_CH).reshape(1, D_CH),
            jnp.tile(k_norm_w[l], N_HEADS_CH).reshape(1, D_CH),
            conv_w[l], onw[D_SB + D_CH:].reshape(1, CONV_CH), g256)
        onw_sb = jnp.broadcast_to(onw[0:D_SB].reshape(D_SB, 1), (D_SB, SB_T))
        ysb = _stickbreak(qat, ka3, vat3, tri, onw_sb)
        ych = _chunkattn(qb, kb, vb, _bias_table(rel_bias[l]), onw[D_SB:D_SB + D_CH].reshape(1, D_CH), g128)
        xs = _out_ffn(xs, ysb, ych, ycv, w_out[l].astype(BF16), ffn_norm_w[l].reshape(1, d),
                      w_gate[l].astype(BF16), w_up[l].astype(BF16), w_down[l].astype(BF16))
    return xs.reshape(b, s, d)
```

```python
import jax
import jax.numpy as jnp
import numpy as np
from jax import lax
from jax.experimental import pallas as pl
from jax.experimental.pallas import tpu as pltpu

F32 = jnp.float32
BF16 = jnp.bfloat16

D_MODEL = 1024
HEAD_DIM = 64
N_HEADS_SB = 4
N_HEADS_CH = 8
CONV_CH = 256
CONV_WIDTH = 3
CHUNK = 64
N_PREV_CHUNKS = 8
BAND = (N_PREV_CHUNKS + 1) * CHUNK
REL_CLIP = 128
D_SB = N_HEADS_SB * HEAD_DIM
D_CH = N_HEADS_CH * HEAD_DIM
D_IN = 3 * D_SB + 3 * D_CH + 3 * CONV_CH
D_FF = 2816
EPS = 1e-6

LOG2E = 1.4426950408889634
QK_SCALE = HEAD_DIM ** -0.5
NEG_BIG = -1e30
SIGN_BIT = np.int32(-2 ** 31)

VMEM_LIMIT_BYTES = 56 * 1024 * 1024

TM_IN = 512
TM_OUT = 512
SB_T = 256
SB_TQ = 512
CH_TQ = 512
CH_SUB = 128
CH_WIN = CH_SUB + N_PREV_CHUNKS * CHUNK


def _dot(a, b):
    return jnp.dot(a, b, preferred_element_type=F32)


def _dot_nt(a, b):
    return lax.dot_general(a, b, (((1,), (1,)), ((), ())), preferred_element_type=F32)


def _group_mean_sq(x, g):
    w = g.shape[0]
    sq = x * x
    hi = sq.astype(BF16)
    lo = (sq - hi.astype(F32)).astype(BF16)
    parts = []
    for c in range(0, x.shape[1], w):
        parts.append(_dot(hi[:, c:c + w], g) + _dot(lo[:, c:c + w], g))
    return parts[0] if len(parts) == 1 else jnp.concatenate(parts, axis=1)


def _in_proj_kernel(x_ref, nw_ref, w_ref, wkt_ref, wvt_ref, qnw_ref, knw_ref, cw_ref, onw_ref, g_ref,
                    qa_ref, kat_ref, vat_ref, qb_ref, kb_ref, vb_ref, ycv_ref, hbuf_ref):
    i = pl.program_id(0)
    tm = x_ref.shape[0]
    xf = x_ref[...]
    ms = jnp.mean(xf * xf, axis=-1, keepdims=True)
    h = ((xf * lax.rsqrt(ms + EPS)) * nw_ref[...]).astype(BF16)

    qa_ref[...] = (_dot(h, w_ref[:, 0:D_SB]) * (QK_SCALE * LOG2E)).astype(BF16)
    kat = _dot_nt(wkt_ref[...], h).astype(BF16)
    vat = _dot_nt(wvt_ref[...], h).astype(BF16)
    for t in range(tm // SB_T):
        kat_ref[t] = kat[:, t * SB_T:(t + 1) * SB_T]
        vat_ref[t] = vat[:, t * SB_T:(t + 1) * SB_T]

    rest = _dot(h, w_ref[:, 3 * D_SB:])
    g = g_ref[...]
    q_b = rest[:, 0:D_CH]
    k_b = rest[:, D_CH:2 * D_CH]
    qn = (q_b * lax.rsqrt(_group_mean_sq(q_b, g) + EPS)) * qnw_ref[...]
    qb_ref[...] = (qn * (QK_SCALE * LOG2E)).astype(BF16)
    kn = (k_b * lax.rsqrt(_group_mean_sq(k_b, g) + EPS)) * knw_ref[...]
    kb_ref[...] = kn.astype(BF16)
    vb_ref[...] = rest[:, 2 * D_CH:3 * D_CH].astype(BF16)

    off = 3 * D_CH
    g_b = rest[:, off:off + CONV_CH]
    hc = rest[:, off + CONV_CH:off + 2 * CONV_CH] * rest[:, off + 2 * CONV_CH:off + 3 * CONV_CH]

    @pl.when(i == 0)
    def _():
        hbuf_ref[0:8, :] = jnp.zeros((8, CONV_CH), F32)

    hbuf_ref[8:8 + tm, :] = hc
    h1 = hbuf_ref[7:7 + tm, :]
    h2 = hbuf_ref[6:6 + tm, :]
    cw = cw_ref[...]
    ycv = g_b * (cw[0:1, :] * h2 + cw[1:2, :] * h1 + cw[2:3, :] * hc)
    hbuf_ref[0:8, :] = hbuf_ref[tm:tm + 8, :]
    yn = (ycv * lax.rsqrt(_group_mean_sq(ycv, g) + EPS)) * onw_ref[...]
    ycv_ref[...] = yn.astype(BF16)


def _in_proj(x, nw, w, wkt, wvt, qnw, knw, cw, onw_cv, g256):
    s = x.shape[0]
    tm = TM_IN
    nk = s // SB_T
    const = lambda shape: pl.BlockSpec(shape, lambda i: (0,) * len(shape), pipeline_mode=pl.Buffered(1))
    return pl.pallas_call(
        _in_proj_kernel,
        grid=(s // tm,),
        in_specs=[
            pl.BlockSpec((tm, D_MODEL), lambda i: (i, 0)),
            const((1, D_MODEL)),
            const((D_MODEL, D_IN)),
            const((D_SB, D_MODEL)),
            const((D_SB, D_MODEL)),
            const((1, D_CH)),
            const((1, D_CH)),
            const((CONV_WIDTH, CONV_CH)),
            const((1, CONV_CH)),
            const((256, 256)),
        ],
        out_specs=[
            pl.BlockSpec((tm, D_SB), lambda i: (i, 0)),
            pl.BlockSpec((tm // SB_T, D_SB, SB_T), lambda i: (i, 0, 0)),
            pl.BlockSpec((tm // SB_T, D_SB, SB_T), lambda i: (i, 0, 0)),
            pl.BlockSpec((tm, D_CH), lambda i: (i, 0)),
            pl.BlockSpec((tm, D_CH), lambda i: (i, 0)),
            pl.BlockSpec((tm, D_CH), lambda i: (i, 0)),
            pl.BlockSpec((tm, CONV_CH), lambda i: (i, 0)),
        ],
        out_shape=[
            jax.ShapeDtypeStruct((s, D_SB), BF16),
            jax.ShapeDtypeStruct((nk, D_SB, SB_T), BF16),
            jax.ShapeDtypeStruct((nk, D_SB, SB_T), BF16),
            jax.ShapeDtypeStruct((s, D_CH), BF16),
            jax.ShapeDtypeStruct((s, D_CH), BF16),
            jax.ShapeDtypeStruct((s, D_CH), BF16),
            jax.ShapeDtypeStruct((s, CONV_CH), BF16),
        ],
        scratch_shapes=[pltpu.VMEM((tm + 8, CONV_CH), F32)],
        compiler_params=pltpu.CompilerParams(
            dimension_semantics=("arbitrary",), vmem_limit_bytes=VMEM_LIMIT_BYTES),
        name="in_proj",
    )(x, nw, w, wkt, wvt, qnw, knw, cw, onw_cv, g256)


def _neg_abs(z):
    return lax.bitcast_convert_type(lax.bitcast_convert_type(z, jnp.int32) | SIGN_BIT, F32)


def _sb_stage_a(j, qm_ref, kat_ref, buf, diag_off=None):
    ls_ref, sp_ref, sp0_ref = buf
    z = _dot(qm_ref[...], kat_ref[j])
    e = jnp.exp2(_neg_abs(z))
    sp = jnp.maximum(z, 0.0) + jnp.log2(1.0 + e)
    ls = z - sp
    if diag_off is not None:
        t_idx = lax.broadcasted_iota(jnp.int32, z.shape, 0) % SB_TQ
        s_idx = lax.broadcasted_iota(jnp.int32, z.shape, 1) + diag_off
        mask = s_idx < t_idx
        sp = jnp.where(mask, sp, 0.0)
        ls = jnp.where(mask, ls, NEG_BIG)
    ls_ref[...] = ls
    sp_ref[...] = sp.astype(BF16)
    sp0_ref[...] = sp[:, 0:1]


def _sb_stage_b(j, buf, vat_ref, ntri, acc_ref, carry_ref, w_ref):
    ls_ref, sp_ref, sp0_ref = buf
    a = _dot(sp_ref[...], ntri) + carry_ref[...]
    w_ref[...] = jnp.exp2(ls_ref[...] + a).astype(BF16)
    carry_ref[...] = a[:, 0:1] - sp0_ref[...]
    vt = vat_ref[j]
    for h in range(N_HEADS_SB):
        acc_ref[h * HEAD_DIM:(h + 1) * HEAD_DIM, :] += _dot_nt(
            vt[h * HEAD_DIM:(h + 1) * HEAD_DIM, :], w_ref[h * SB_TQ:(h + 1) * SB_TQ, :])


def _sb_kernel(qa_ref, kat_ref, vat_ref, ntri_ref, onw_ref, o_ref, qm_ref, acc_ref, carry_ref,
               ls0_ref, sp0_ref, c0_ref, ls1_ref, sp1_ref, c1_ref, w_ref):
    i = pl.program_id(0)
    q = qa_ref[...]
    lane_head = lax.broadcasted_iota(jnp.int32, q.shape, 1) // HEAD_DIM
    for h in range(N_HEADS_SB):
        qm_ref[h * SB_TQ:(h + 1) * SB_TQ, :] = jnp.where(lane_head == h, q, jnp.zeros_like(q))
    acc_ref[...] = jnp.zeros_like(acc_ref)
    carry_ref[...] = jnp.zeros_like(carry_ref)
    ntri = ntri_ref[...]
    bufs = ((ls0_ref, sp0_ref, c0_ref), (ls1_ref, sp1_ref, c1_ref))
    stage_a = lambda j, b, off=None: _sb_stage_a(j, qm_ref, kat_ref, bufs[b], off)
    stage_b = lambda j, b: _sb_stage_b(j, bufs[b], vat_ref, ntri, acc_ref, carry_ref, w_ref)

    r = SB_TQ // SB_T
    stage_a(r * i + r - 1, 0, (r - 1) * SB_T)
    for d in range(1, r):
        stage_a(r * i + r - 1 - d, d % 2, (r - 1 - d) * SB_T)
        stage_b(r * i + r - d, (d - 1) % 2)
    b0 = (r - 1) % 2

    def body(p, c):
        j = r * i - 2 * p
        stage_a(j - 1, 1 - b0)
        stage_b(j, b0)
        stage_a(j - 2, b0)
        stage_b(j - 1, 1 - b0)
        return c

    lax.fori_loop(0, (r * i) // 2, body, 0)
    stage_b(0, b0)

    acc = acc_ref[...]
    parts = []
    for h in range(N_HEADS_SB):
        a = acc[h * HEAD_DIM:(h + 1) * HEAD_DIM, :]
        ms = jnp.mean(a * a, axis=0, keepdims=True)
        parts.append(a * lax.rsqrt(ms + EPS))
    y = jnp.concatenate(parts, axis=0) * onw_ref[...]
    o_ref[...] = y.T.astype(BF16)


def _stickbreak(qa, kat3, vat3, ntri, onw_b):
    nk = kat3.shape[0]
    s = nk * SB_T
    rows = N_HEADS_SB * SB_TQ
    const = lambda shape: pl.BlockSpec(shape, lambda i: (0,) * len(shape), pipeline_mode=pl.Buffered(1))
    tile_bufs = [pltpu.VMEM((rows, SB_T), F32), pltpu.VMEM((rows, SB_T), BF16), pltpu.VMEM((rows, 1), F32)]
    return pl.pallas_call(
        _sb_kernel,
        grid=(s // SB_TQ,),
        in_specs=[
            pl.BlockSpec((SB_TQ, D_SB), lambda i: (i, 0)),
            const((nk, D_SB, SB_T)),
            const((nk, D_SB, SB_T)),
            const((SB_T, SB_T)),
            const((D_SB, SB_TQ)),
        ],
        out_specs=pl.BlockSpec((SB_TQ, D_SB), lambda i: (i, 0)),
        out_shape=jax.ShapeDtypeStruct((s, D_SB), BF16),
        scratch_shapes=[
            pltpu.VMEM((rows, D_SB), BF16),
            pltpu.VMEM((D_SB, SB_TQ), F32),
            pltpu.VMEM((rows, 1), F32),
            *tile_bufs, *tile_bufs,
            pltpu.VMEM((rows, SB_T), BF16),
        ],
        compiler_params=pltpu.CompilerParams(
            dimension_semantics=("arbitrary",), vmem_limit_bytes=VMEM_LIMIT_BYTES),
        name="stickbreak",
    )(qa, kat3, vat3, ntri, onw_b)


def _chunk_kernel(q_ref, kp_ref, kc_ref, vp_ref, vc_ref, gb_ref, onw_ref, o_ref, kbuf_ref, vbuf_ref, bias_ref):
    i = pl.program_id(0)
    tq = q_ref.shape[0]

    @pl.when(i == 0)
    def _():
        lane_w = lax.broadcasted_iota(jnp.int32, (CHUNK, CH_WIN), 1)
        for hd in range(N_HEADS_CH):
            base = jnp.broadcast_to(gb_ref[hd:hd + 1, :], (CHUNK, CH_WIN))
            first = pltpu.roll(base, CH_WIN - (CHUNK - 1), 1, stride=1, stride_axis=0)
            second = pltpu.roll(base, 1, 1, stride=1, stride_axis=0)
            bias_ref[hd, 0:CHUNK, :] = jnp.where(lane_w < BAND, first, NEG_BIG)
            bias_ref[hd, CHUNK:2 * CHUNK, :] = jnp.where(lane_w >= CHUNK, second, NEG_BIG)

    kbuf_ref[0:tq, :] = kp_ref[...]
    kbuf_ref[tq:2 * tq, :] = kc_ref[...]
    vbuf_ref[0:tq, :] = vp_ref[...]
    vbuf_ref[tq:2 * tq, :] = vc_ref[...]
    col = lax.broadcasted_iota(jnp.int32, (1, 2 * tq), 1)
    kmask = jnp.where(jnp.logical_and(col < tq, i == 0), NEG_BIG, 0.0).astype(F32)
    lane = lax.broadcasted_iota(jnp.int32, (CH_SUB, 2 * HEAD_DIM), 1)
    low = lane < HEAD_DIM
    n_pairs = N_HEADS_CH // 2
    for sub in range(tq // CH_SUB):
        r0 = sub * CH_SUB
        km = kmask[:, r0:r0 + CH_WIN]
        scores = []
        for p in range(n_pairs):
            c0 = p * 2 * HEAD_DIM
            qp = q_ref[r0:r0 + CH_SUB, c0:c0 + 2 * HEAD_DIM]
            qm = jnp.concatenate([jnp.where(low, qp, jnp.zeros_like(qp)),
                                  jnp.where(low, jnp.zeros_like(qp), qp)], axis=0)
            sc = _dot_nt(qm, kbuf_ref[r0:r0 + CH_WIN, c0:c0 + 2 * HEAD_DIM])
            bias = jnp.concatenate([bias_ref[2 * p], bias_ref[2 * p + 1]], axis=0)
            scores.append(sc + bias + km)
        probs, inv_l = [], []
        for p in range(n_pairs):
            m = jnp.max(scores[p], axis=-1, keepdims=True)
            pr = jnp.exp2(scores[p] - m)
            inv_l.append(1.0 / jnp.sum(pr, axis=-1, keepdims=True))
            probs.append(pr.astype(BF16))
        outs = []
        for p in range(n_pairs):
            c0 = p * 2 * HEAD_DIM
            o2 = _dot(probs[p], vbuf_ref[r0:r0 + CH_WIN, c0:c0 + 2 * HEAD_DIM]) * inv_l[p]
            outs.append(jnp.where(low, o2[0:CH_SUB], o2[CH_SUB:2 * CH_SUB]))
        for p in range(n_pairs):
            c0 = p * 2 * HEAD_DIM
            o = outs[p]
            sq = o * o
            ms = jnp.where(low,
                           jnp.sum(jnp.where(low, sq, 0.0), axis=-1, keepdims=True),
                           jnp.sum(jnp.where(low, 0.0, sq), axis=-1, keepdims=True)) * (1.0 / HEAD_DIM)
            y = (o * lax.rsqrt(ms + EPS)) * onw_ref[:, c0:c0 + 2 * HEAD_DIM]
            o_ref[r0:r0 + CH_SUB, c0:c0 + 2 * HEAD_DIM] = y.astype(BF16)


def _chunkattn(qb, kb, vb, gb, onw_ch):
    s = qb.shape[0]
    tq = CH_TQ
    const = lambda shape: pl.BlockSpec(shape, lambda i: (0,) * len(shape), pipeline_mode=pl.Buffered(1))
    cur = pl.BlockSpec((tq, D_CH), lambda i: (i, 0))
    prev = pl.BlockSpec((tq, D_CH), lambda i: (jnp.maximum(i - 1, 0), 0))
    return pl.pallas_call(
        _chunk_kernel,
        grid=(s // tq,),
        in_specs=[cur, prev, cur, prev, cur,
                  const((N_HEADS_CH, CH_WIN)), const((1, D_CH))],
        out_specs=pl.BlockSpec((tq, D_CH), lambda i: (i, 0)),
        out_shape=jax.ShapeDtypeStruct((s, D_CH), BF16),
        scratch_shapes=[pltpu.VMEM((2 * tq, D_CH), BF16), pltpu.VMEM((2 * tq, D_CH), BF16),
                        pltpu.VMEM((N_HEADS_CH, CH_SUB, CH_WIN), F32)],
        compiler_params=pltpu.CompilerParams(
            dimension_semantics=("arbitrary",), vmem_limit_bytes=VMEM_LIMIT_BYTES),
        name="chunkattn",
    )(qb, kb, kb, vb, vb, gb, onw_ch)


def _out_ffn_kernel(x_ref, ysb_ref, ych_ref, ycv_ref, wo_ref, fnw_ref, wg_ref, wu_ref, wd_ref, o_ref):
    x1 = (x_ref[...]
          + _dot(ysb_ref[...], wo_ref[0:D_SB, :])
          + _dot(ych_ref[...], wo_ref[D_SB:D_SB + D_CH, :])
          + _dot(ycv_ref[...], wo_ref[D_SB + D_CH:, :]))
    ms = jnp.mean(x1 * x1, axis=-1, keepdims=True)
    h = ((x1 * lax.rsqrt(ms + EPS)) * fnw_ref[...]).astype(BF16)
    gate = _dot(h, wg_ref[...])
    up = _dot(h, wu_ref[...])
    act = ((gate * (1.0 / (1.0 + jnp.exp(-gate)))) * up).astype(BF16)
    o_ref[...] = x1 + _dot(act, wd_ref[...])


def _out_ffn(x, ysb, ych, ycv, wo, fnw, wg, wu, wd):
    s = x.shape[0]
    tm = TM_OUT
    const = lambda shape: pl.BlockSpec(shape, lambda i: (0,) * len(shape), pipeline_mode=pl.Buffered(1))
    row = lambda w: pl.BlockSpec((tm, w), lambda i: (i, 0))
    return pl.pallas_call(
        _out_ffn_kernel,
        grid=(s // tm,),
        in_specs=[row(D_MODEL), row(D_SB), row(D_CH), row(CONV_CH),
                  const((D_MODEL, D_MODEL)), const((1, D_MODEL)),
                  const((D_MODEL, D_FF)), const((D_MODEL, D_FF)), const((D_FF, D_MODEL))],
        out_specs=row(D_MODEL),
        out_shape=jax.ShapeDtypeStruct((s, D_MODEL), F32),
        compiler_params=pltpu.CompilerParams(
            dimension_semantics=("arbitrary",), vmem_limit_bytes=VMEM_LIMIT_BYTES),
        name="out_ffn",
    )(x, ysb, ych, ycv, wo, fnw, wg, wu, wd)


def _block_diag_mean(width):
    idx = np.arange(width) // HEAD_DIM
    return jnp.asarray((idx[:, None] == idx[None, :]).astype(np.float32) / HEAD_DIM, dtype=BF16)


def _bias_base_index():
    v = np.arange(CH_WIN)
    u = np.minimum(v - (CHUNK - 1), BAND - 1)
    return np.clip(N_PREV_CHUNKS * CHUNK - u, -REL_CLIP, REL_CLIP) + REL_CLIP


def kernel(x, attn_norm_w, w_in, q_norm_w, k_norm_w, rel_bias, conv_w, out_norm_w, w_out, ffn_norm_w,
           w_gate, w_up, w_down):
    b, s, d = x.shape
    assert b == 1 and d == D_MODEL and s % TM_IN == 0 and s % TM_OUT == 0 and s % CH_TQ == 0 and s % SB_TQ == 0
    depth = w_in.shape[0]
    xs = x.reshape(s, d)
    g256 = _block_diag_mean(256)
    ar = np.arange(SB_T)
    ntri = jnp.asarray(-(ar[:, None] > ar[None, :]).astype(np.float32), dtype=BF16)
    for l in range(depth):
        w_l = w_in[l].astype(BF16)
        wkt = w_in[l][:, D_SB:2 * D_SB].T.astype(BF16)
        wvt = w_in[l][:, 2 * D_SB:3 * D_SB].T.astype(BF16)
        onw = out_norm_w[l].astype(F32)
        qa, kat3, vat3, qb, kb, vb, ycv = _in_proj(
            xs, attn_norm_w[l].reshape(1, d), w_l, wkt, wvt,
            jnp.tile(q_norm_w[l], N_HEADS_CH).reshape(1, D_CH),
            jnp.tile(k_norm_w[l], N_HEADS_CH).reshape(1, D_CH),
            conv_w[l], onw[D_SB + D_CH:].reshape(1, CONV_CH), g256)
        onw_sb = jnp.broadcast_to(onw[0:D_SB].reshape(D_SB, 1), (D_SB, SB_TQ))
        ysb = _stickbreak(qa, kat3, vat3, ntri, onw_sb)
        gb = rel_bias[l].astype(F32)[:, _bias_base_index()] * LOG2E
        ych = _chunkattn(qb, kb, vb, gb, onw[D_SB:D_SB + D_CH].reshape(1, D_CH))
        xs = _out_ffn(xs, ysb, ych, ycv, w_out[l].astype(BF16), ffn_norm_w[l].reshape(1, d),
                      w_gate[l].astype(BF16), w_up[l].astype(BF16), w_down[l].astype(BF16))
    return xs.reshape(b, s, d)
```

```python
import jax
import jax.numpy as jnp
import numpy as np
from jax import lax
from jax.experimental import pallas as pl
from jax.experimental.pallas import tpu as pltpu

F32 = jnp.float32
BF16 = jnp.bfloat16

D_MODEL = 1024
HEAD_DIM = 64
N_HEADS_SB = 4
N_HEADS_CH = 8
CONV_CH = 256
CONV_WIDTH = 3
CHUNK = 64
N_PREV_CHUNKS = 8
BAND = (N_PREV_CHUNKS + 1) * CHUNK
REL_CLIP = 128
D_SB = N_HEADS_SB * HEAD_DIM
D_CH = N_HEADS_CH * HEAD_DIM
D_IN = 3 * D_SB + 3 * D_CH + 3 * CONV_CH
D_FF = 2816
EPS = 1e-6

LOG2E = 1.4426950408889634
QK_SCALE = HEAD_DIM ** -0.5
NEG_BIG = -1e30

VMEM_LIMIT_BYTES = 56 * 1024 * 1024

TM_IN = 512
TM_OUT = 512
SB_T = 256
SB_TQ = 512
CH_TQ = 512
CH_SUB = 128
CH_WIN = CH_SUB + N_PREV_CHUNKS * CHUNK


def _dot(a, b):
    return jnp.dot(a, b, preferred_element_type=F32)


def _dot_nt(a, b):
    return lax.dot_general(a, b, (((1,), (1,)), ((), ())), preferred_element_type=F32)


def _group_mean_sq(x):
    low = lax.broadcasted_iota(jnp.int32, (x.shape[0], 2 * HEAD_DIM), 1) < HEAD_DIM
    parts = []
    for c in range(0, x.shape[1], 2 * HEAD_DIM):
        sq = x[:, c:c + 2 * HEAD_DIM]
        sq = sq * sq
        lo = jnp.sum(jnp.where(low, sq, 0.0), axis=-1, keepdims=True)
        hi = jnp.sum(jnp.where(low, 0.0, sq), axis=-1, keepdims=True)
        parts.append(jnp.where(low, lo, hi) * (1.0 / HEAD_DIM))
    return parts[0] if len(parts) == 1 else jnp.concatenate(parts, axis=1)


def _in_proj_kernel(x_ref, nw_ref, w_ref, wkt_ref, wvt_ref, qnw_ref, knw_ref, cw_ref, onw_ref,
                    qa_ref, kat_ref, vat_ref, qb_ref, kb_ref, vb_ref, ycv_ref, hbuf_ref):
    i = pl.program_id(0)
    tm = x_ref.shape[0]
    xf = x_ref[...]
    ms = jnp.mean(xf * xf, axis=-1, keepdims=True)
    h = ((xf * lax.rsqrt(ms + EPS)) * nw_ref[...]).astype(BF16)

    qa_ref[...] = (_dot(h, w_ref[:, 0:D_SB]) * (QK_SCALE * LOG2E)).astype(BF16)
    kat = _dot_nt(wkt_ref[...], h).astype(BF16)
    vat = _dot_nt(wvt_ref[...], h).astype(BF16)
    for t in range(tm // SB_T):
        kat_ref[t] = kat[:, t * SB_T:(t + 1) * SB_T]
        vat_ref[t] = vat[:, t * SB_T:(t + 1) * SB_T]

    rest = _dot(h, w_ref[:, 3 * D_SB:])
    q_b = rest[:, 0:D_CH]
    k_b = rest[:, D_CH:2 * D_CH]
    qn = (q_b * lax.rsqrt(_group_mean_sq(q_b) + EPS)) * qnw_ref[...]
    qb_ref[...] = (qn * (QK_SCALE * LOG2E)).astype(BF16)
    kn = (k_b * lax.rsqrt(_group_mean_sq(k_b) + EPS)) * knw_ref[...]
    kb_ref[...] = kn.astype(BF16)
    vb_ref[...] = rest[:, 2 * D_CH:3 * D_CH].astype(BF16)

    off = 3 * D_CH
    g_b = rest[:, off:off + CONV_CH]
    hc = rest[:, off + CONV_CH:off + 2 * CONV_CH] * rest[:, off + 2 * CONV_CH:off + 3 * CONV_CH]

    @pl.when(i == 0)
    def _():
        hbuf_ref[0:8, :] = jnp.zeros((8, CONV_CH), F32)

    hbuf_ref[8:8 + tm, :] = hc
    h1 = hbuf_ref[7:7 + tm, :]
    h2 = hbuf_ref[6:6 + tm, :]
    cw = cw_ref[...]
    ycv = g_b * (cw[0:1, :] * h2 + cw[1:2, :] * h1 + cw[2:3, :] * hc)
    hbuf_ref[0:8, :] = hbuf_ref[tm:tm + 8, :]
    yn = (ycv * lax.rsqrt(_group_mean_sq(ycv) + EPS)) * onw_ref[...]
    ycv_ref[...] = yn.astype(BF16)


def _in_proj(x, nw, w, wkt, wvt, qnw, knw, cw, onw_cv):
    s = x.shape[0]
    tm = TM_IN
    nk = s // SB_T
    const = lambda shape: pl.BlockSpec(shape, lambda i: (0,) * len(shape), pipeline_mode=pl.Buffered(1))
    return pl.pallas_call(
        _in_proj_kernel,
        grid=(s // tm,),
        in_specs=[
            pl.BlockSpec((tm, D_MODEL), lambda i: (i, 0)),
            const((1, D_MODEL)),
            const((D_MODEL, D_IN)),
            const((D_SB, D_MODEL)),
            const((D_SB, D_MODEL)),
            const((1, D_CH)),
            const((1, D_CH)),
            const((CONV_WIDTH, CONV_CH)),
            const((1, CONV_CH)),
        ],
        out_specs=[
            pl.BlockSpec((tm, D_SB), lambda i: (i, 0)),
            pl.BlockSpec((tm // SB_T, D_SB, SB_T), lambda i: (i, 0, 0)),
            pl.BlockSpec((tm // SB_T, D_SB, SB_T), lambda i: (i, 0, 0)),
            pl.BlockSpec((tm, D_CH), lambda i: (i, 0)),
            pl.BlockSpec((tm, D_CH), lambda i: (i, 0)),
            pl.BlockSpec((tm, D_CH), lambda i: (i, 0)),
            pl.BlockSpec((tm, CONV_CH), lambda i: (i, 0)),
        ],
        out_shape=[
            jax.ShapeDtypeStruct((s, D_SB), BF16),
            jax.ShapeDtypeStruct((nk, D_SB, SB_T), BF16),
            jax.ShapeDtypeStruct((nk, D_SB, SB_T), BF16),
            jax.ShapeDtypeStruct((s, D_CH), BF16),
            jax.ShapeDtypeStruct((s, D_CH), BF16),
            jax.ShapeDtypeStruct((s, D_CH), BF16),
            jax.ShapeDtypeStruct((s, CONV_CH), BF16),
        ],
        scratch_shapes=[pltpu.VMEM((tm + 8, CONV_CH), F32)],
        compiler_params=pltpu.CompilerParams(
            dimension_semantics=("arbitrary",), vmem_limit_bytes=VMEM_LIMIT_BYTES),
        name="in_proj",
    )(x, nw, w, wkt, wvt, qnw, knw, cw, onw_cv)


def _sb_stage_a(j, qm_ref, kat_ref, buf, diag_off=None):
    ls_ref, sp_ref, sp0_ref = buf
    z = _dot(qm_ref[...], kat_ref[j])
    e = jnp.exp2(-jnp.abs(z))
    sp = jnp.maximum(z, 0.0) + jnp.log2(1.0 + e)
    ls = z - sp
    if diag_off is not None:
        t_idx = lax.broadcasted_iota(jnp.int32, z.shape, 0) % SB_TQ
        s_idx = lax.broadcasted_iota(jnp.int32, z.shape, 1) + diag_off
        mask = s_idx < t_idx
        sp = jnp.where(mask, sp, 0.0)
        ls = jnp.where(mask, ls, NEG_BIG)
    ls_ref[...] = ls
    sp_ref[...] = sp.astype(BF16)
    sp0_ref[...] = sp[:, 0:1]


def _sb_stage_b(j, buf, vat_ref, ntri, acc_ref, carry_ref, w_ref):
    ls_ref, sp_ref, sp0_ref = buf
    a = _dot(sp_ref[...], ntri) + carry_ref[...]
    w_ref[...] = jnp.exp2(ls_ref[...] + a).astype(BF16)
    carry_ref[...] = a[:, 0:1] - sp0_ref[...]
    vt = vat_ref[j]
    for h in range(N_HEADS_SB):
        acc_ref[h * HEAD_DIM:(h + 1) * HEAD_DIM, :] += _dot_nt(
            vt[h * HEAD_DIM:(h + 1) * HEAD_DIM, :], w_ref[h * SB_TQ:(h + 1) * SB_TQ, :])


def _sb_kernel(qa_ref, kat_ref, vat_ref, ntri_ref, onw_ref, o_ref, qm_ref, acc_ref, carry_ref,
               ls0_ref, sp0_ref, c0_ref, ls1_ref, sp1_ref, c1_ref, w_ref):
    i = pl.program_id(0)
    q = qa_ref[...]
    lane_head = lax.broadcasted_iota(jnp.int32, q.shape, 1) // HEAD_DIM
    for h in range(N_HEADS_SB):
        qm_ref[h * SB_TQ:(h + 1) * SB_TQ, :] = jnp.where(lane_head == h, q, jnp.zeros_like(q))
    acc_ref[...] = jnp.zeros_like(acc_ref)
    carry_ref[...] = jnp.zeros_like(carry_ref)
    ntri = ntri_ref[...]
    bufs = ((ls0_ref, sp0_ref, c0_ref), (ls1_ref, sp1_ref, c1_ref))
    stage_a = lambda j, b, off=None: _sb_stage_a(j, qm_ref, kat_ref, bufs[b], off)
    stage_b = lambda j, b: _sb_stage_b(j, bufs[b], vat_ref, ntri, acc_ref, carry_ref, w_ref)

    r = SB_TQ // SB_T
    stage_a(r * i + r - 1, 0, (r - 1) * SB_T)
    for d in range(1, r):
        stage_a(r * i + r - 1 - d, d % 2, (r - 1 - d) * SB_T)
        stage_b(r * i + r - d, (d - 1) % 2)
    b0 = (r - 1) % 2

    n_pairs = (r * i) // 2

    def pair(j):
        stage_a(j - 1, 1 - b0)
        stage_b(j, b0)
        stage_a(j - 2, b0)
        stage_b(j - 1, 1 - b0)

    @pl.when(n_pairs % 2 == 1)
    def _():
        pair(r * i)

    def body(p, c):
        j = r * i - 2 * (n_pairs % 2) - 4 * p
        pair(j)
        pair(j - 2)
        return c

    lax.fori_loop(0, n_pairs // 2, body, 0)
    stage_b(0, b0)

    acc = acc_ref[...]
    parts = []
    for h in range(N_HEADS_SB):
        a = acc[h * HEAD_DIM:(h + 1) * HEAD_DIM, :]
        ms = jnp.mean(a * a, axis=0, keepdims=True)
        parts.append(a * lax.rsqrt(ms + EPS))
    y = jnp.concatenate(parts, axis=0) * onw_ref[...]
    o_ref[...] = y.T.astype(BF16)


def _stickbreak(qa, kat3, vat3, ntri, onw_b):
    nk = kat3.shape[0]
    s = nk * SB_T
    rows = N_HEADS_SB * SB_TQ
    const = lambda shape: pl.BlockSpec(shape, lambda i: (0,) * len(shape), pipeline_mode=pl.Buffered(1))
    tile_bufs = [pltpu.VMEM((rows, SB_T), F32), pltpu.VMEM((rows, SB_T), BF16), pltpu.VMEM((rows, 1), F32)]
    return pl.pallas_call(
        _sb_kernel,
        grid=(s // SB_TQ,),
        in_specs=[
            pl.BlockSpec((SB_TQ, D_SB), lambda i: (i, 0)),
            const((nk, D_SB, SB_T)),
            const((nk, D_SB, SB_T)),
            const((SB_T, SB_T)),
            const((D_SB, SB_TQ)),
        ],
        out_specs=pl.BlockSpec((SB_TQ, D_SB), lambda i: (i, 0)),
        out_shape=jax.ShapeDtypeStruct((s, D_SB), BF16),
        scratch_shapes=[
            pltpu.VMEM((rows, D_SB), BF16),
            pltpu.VMEM((D_SB, SB_TQ), F32),
            pltpu.VMEM((rows, 1), F32),
            *tile_bufs, *tile_bufs,
            pltpu.VMEM((rows, SB_T), BF16),
        ],
        compiler_params=pltpu.CompilerParams(
            dimension_semantics=("arbitrary",), vmem_limit_bytes=VMEM_LIMIT_BYTES),
        name="stickbreak",
    )(qa, kat3, vat3, ntri, onw_b)


def _chunk_kernel(q_ref, kp_ref, kc_ref, vp_ref, vc_ref, gb_ref, onw_ref, o_ref, kbuf_ref, vbuf_ref, bias_ref):
    i = pl.program_id(0)
    tq = q_ref.shape[0]

    @pl.when(i == 0)
    def _():
        lane_w = lax.broadcasted_iota(jnp.int32, (CHUNK, CH_WIN), 1)
        for hd in range(N_HEADS_CH):
            base = jnp.broadcast_to(gb_ref[hd:hd + 1, :], (CHUNK, CH_WIN))
            first = pltpu.roll(base, CH_WIN - (CHUNK - 1), 1, stride=1, stride_axis=0)
            second = pltpu.roll(base, 1, 1, stride=1, stride_axis=0)
            bias_ref[hd, 0:CHUNK, :] = jnp.where(lane_w < BAND, first, NEG_BIG)
            bias_ref[hd, CHUNK:2 * CHUNK, :] = jnp.where(lane_w >= CHUNK, second, NEG_BIG)

    kbuf_ref[0:tq, :] = kp_ref[...]
    kbuf_ref[tq:2 * tq, :] = kc_ref[...]
    vbuf_ref[0:tq, :] = vp_ref[...]
    vbuf_ref[tq:2 * tq, :] = vc_ref[...]
    col = lax.broadcasted_iota(jnp.int32, (1, 2 * tq), 1)
    kmask = jnp.where(jnp.logical_and(col < tq, i == 0), NEG_BIG, 0.0).astype(F32)
    lane = lax.broadcasted_iota(jnp.int32, (CH_SUB, 2 * HEAD_DIM), 1)
    low = lane < HEAD_DIM
    n_pairs = N_HEADS_CH // 2
    for sub in range(tq // CH_SUB):
        r0 = sub * CH_SUB
        km = kmask[:, r0:r0 + CH_WIN]
        scores = []
        for p in range(n_pairs):
            c0 = p * 2 * HEAD_DIM
            qp = q_ref[r0:r0 + CH_SUB, c0:c0 + 2 * HEAD_DIM]
            qm = jnp.concatenate([jnp.where(low, qp, jnp.zeros_like(qp)),
                                  jnp.where(low, jnp.zeros_like(qp), qp)], axis=0)
            sc = _dot_nt(qm, kbuf_ref[r0:r0 + CH_WIN, c0:c0 + 2 * HEAD_DIM])
            bias = jnp.concatenate([bias_ref[2 * p], bias_ref[2 * p + 1]], axis=0)
            scores.append(sc + bias + km)
        probs, inv_l = [], []
        for p in range(n_pairs):
            m = jnp.max(scores[p], axis=-1, keepdims=True)
            pr = jnp.exp2(scores[p] - m)
            inv_l.append(1.0 / jnp.sum(pr, axis=-1, keepdims=True))
            probs.append(pr.astype(BF16))
        outs = []
        for p in range(n_pairs):
            c0 = p * 2 * HEAD_DIM
            o2 = _dot(probs[p], vbuf_ref[r0:r0 + CH_WIN, c0:c0 + 2 * HEAD_DIM]) * inv_l[p]
            outs.append(jnp.where(low, o2[0:CH_SUB], o2[CH_SUB:2 * CH_SUB]))
        for p in range(n_pairs):
            c0 = p * 2 * HEAD_DIM
            o = outs[p]
            y = (o * lax.rsqrt(_group_mean_sq(o) + EPS)) * onw_ref[:, c0:c0 + 2 * HEAD_DIM]
            o_ref[r0:r0 + CH_SUB, c0:c0 + 2 * HEAD_DIM] = y.astype(BF16)


def _chunkattn(qb, kb, vb, gb, onw_ch):
    s = qb.shape[0]
    tq = CH_TQ
    const = lambda shape: pl.BlockSpec(shape, lambda i: (0,) * len(shape), pipeline_mode=pl.Buffered(1))
    cur = pl.BlockSpec((tq, D_CH), lambda i: (i, 0))
    prev = pl.BlockSpec((tq, D_CH), lambda i: (jnp.maximum(i - 1, 0), 0))
    return pl.pallas_call(
        _chunk_kernel,
        grid=(s // tq,),
        in_specs=[cur, prev, cur, prev, cur,
                  const((N_HEADS_CH, CH_WIN)), const((1, D_CH))],
        out_specs=pl.BlockSpec((tq, D_CH), lambda i: (i, 0)),
        out_shape=jax.ShapeDtypeStruct((s, D_CH), BF16),
        scratch_shapes=[pltpu.VMEM((2 * tq, D_CH), BF16), pltpu.VMEM((2 * tq, D_CH), BF16),
                        pltpu.VMEM((N_HEADS_CH, CH_SUB, CH_WIN), F32)],
        compiler_params=pltpu.CompilerParams(
            dimension_semantics=("arbitrary",), vmem_limit_bytes=VMEM_LIMIT_BYTES),
        name="chunkattn",
    )(qb, kb, kb, vb, vb, gb, onw_ch)


def _out_ffn_kernel(x_ref, ysb_ref, ych_ref, ycv_ref, wo_ref, fnw_ref, wg_ref, wu_ref, wd_ref, o_ref):
    x1 = (x_ref[...]
          + _dot(ysb_ref[...], wo_ref[0:D_SB, :])
          + _dot(ych_ref[...], wo_ref[D_SB:D_SB + D_CH, :])
          + _dot(ycv_ref[...], wo_ref[D_SB + D_CH:, :]))
    ms = jnp.mean(x1 * x1, axis=-1, keepdims=True)
    h = ((x1 * lax.rsqrt(ms + EPS)) * fnw_ref[...]).astype(BF16)
    gate = _dot(h, wg_ref[...])
    up = _dot(h, wu_ref[...])
    act = ((gate * (1.0 / (1.0 + jnp.exp(-gate)))) * up).astype(BF16)
    o_ref[...] = x1 + _dot(act, wd_ref[...])


def _out_ffn(x, ysb, ych, ycv, wo, fnw, wg, wu, wd):
    s = x.shape[0]
    tm = TM_OUT
    const = lambda shape: pl.BlockSpec(shape, lambda i: (0,) * len(shape), pipeline_mode=pl.Buffered(1))
    row = lambda w: pl.BlockSpec((tm, w), lambda i: (i, 0))
    return pl.pallas_call(
        _out_ffn_kernel,
        grid=(s // tm,),
        in_specs=[row(D_MODEL), row(D_SB), row(D_CH), row(CONV_CH),
                  const((D_MODEL, D_MODEL)), const((1, D_MODEL)),
                  const((D_MODEL, D_FF)), const((D_MODEL, D_FF)), const((D_FF, D_MODEL))],
        out_specs=row(D_MODEL),
        out_shape=jax.ShapeDtypeStruct((s, D_MODEL), F32),
        compiler_params=pltpu.CompilerParams(
            dimension_semantics=("arbitrary",), vmem_limit_bytes=VMEM_LIMIT_BYTES),
        name="out_ffn",
    )(x, ysb, ych, ycv, wo, fnw, wg, wu, wd)


def _bias_base_index():
    v = np.arange(CH_WIN)
    u = np.minimum(v - (CHUNK - 1), BAND - 1)
    return np.clip(N_PREV_CHUNKS * CHUNK - u, -REL_CLIP, REL_CLIP) + REL_CLIP


def kernel(x, attn_norm_w, w_in, q_norm_w, k_norm_w, rel_bias, conv_w, out_norm_w, w_out, ffn_norm_w,
           w_gate, w_up, w_down):
    b, s, d = x.shape
    assert b == 1 and d == D_MODEL and s % TM_IN == 0 and s % TM_OUT == 0 and s % CH_TQ == 0 and s % SB_TQ == 0
    depth = w_in.shape[0]
    xs = x.reshape(s, d)
    ar = np.arange(SB_T)
    ntri = jnp.asarray(-(ar[:, None] > ar[None, :]).astype(np.float32), dtype=BF16)
    for l in range(depth):
        w_l = w_in[l].astype(BF16)
        wkt = w_in[l][:, D_SB:2 * D_SB].T.astype(BF16)
        wvt = w_in[l][:, 2 * D_SB:3 * D_SB].T.astype(BF16)
        onw = out_norm_w[l].astype(F32)
        qa, kat3, vat3, qb, kb, vb, ycv = _in_proj(
            xs, attn_norm_w[l].reshape(1, d), w_l, wkt, wvt,
            jnp.tile(q_norm_w[l], N_HEADS_CH).reshape(1, D_CH),
            jnp.tile(k_norm_w[l], N_HEADS_CH).reshape(1, D_CH),
            conv_w[l], onw[D_SB + D_CH:].reshape(1, CONV_CH))
        onw_sb = jnp.broadcast_to(onw[0:D_SB].reshape(D_SB, 1), (D_SB, SB_TQ))
        ysb = _stickbreak(qa, kat3, vat3, ntri, onw_sb)
        gb = rel_bias[l].astype(F32)[:, _bias_base_index()] * LOG2E
        ych = _chunkattn(qb, kb, vb, gb, onw[D_SB:D_SB + D_CH].reshape(1, D_CH))
        xs = _out_ffn(xs, ysb, ych, ycv, w_out[l].astype(BF16), ffn_norm_w[l].reshape(1, d),
                      w_gate[l].astype(BF16), w_up[l].astype(BF16), w_down[l].astype(BF16))
    return xs.reshape(b, s, d)
```

```python
import jax
import jax.numpy as jnp
import numpy as np
from jax import lax
from jax.experimental import pallas as pl
from jax.experimental.pallas import tpu as pltpu

F32 = jnp.float32
BF16 = jnp.bfloat16

D_MODEL = 1024
HEAD_DIM = 64
N_HEADS_SB = 4
N_HEADS_CH = 8
CONV_CH = 256
CONV_WIDTH = 3
CHUNK = 64
N_PREV_CHUNKS = 8
BAND = (N_PREV_CHUNKS + 1) * CHUNK
REL_CLIP = 128
D_SB = N_HEADS_SB * HEAD_DIM
D_CH = N_HEADS_CH * HEAD_DIM
D_IN = 3 * D_SB + 3 * D_CH + 3 * CONV_CH
D_FF = 2816
EPS = 1e-6

LOG2E = 1.4426950408889634
QK_SCALE = HEAD_DIM ** -0.5
NEG_BIG = -1e30

VMEM_LIMIT_BYTES = 56 * 1024 * 1024

TM_IN = 1024
IN_SPLIT = 2
TM_OUT = 512
SB_T = 256
SB_TQ = 512
CH_TQ = 512
CH_SUB = 128
CH_WIN = CH_SUB + N_PREV_CHUNKS * CHUNK


def _dot(a, b):
    return jnp.dot(a, b, preferred_element_type=F32)


def _dot_nt(a, b):
    return lax.dot_general(a, b, (((1,), (1,)), ((), ())), preferred_element_type=F32)


def _group_mean_sq(x):
    low = lax.broadcasted_iota(jnp.int32, (x.shape[0], 2 * HEAD_DIM), 1) < HEAD_DIM
    parts = []
    for c in range(0, x.shape[1], 2 * HEAD_DIM):
        sq = x[:, c:c + 2 * HEAD_DIM]
        sq = sq * sq
        lo = jnp.sum(jnp.where(low, sq, 0.0), axis=-1, keepdims=True)
        hi = jnp.sum(jnp.where(low, 0.0, sq), axis=-1, keepdims=True)
        parts.append(jnp.where(low, lo, hi) * (1.0 / HEAD_DIM))
    return parts[0] if len(parts) == 1 else jnp.concatenate(parts, axis=1)


def _in_proj_kernel(x_ref, nw_ref, w_ref, qnw_ref, knw_ref, cw_ref, onw_ref,
                    qa_ref, kat_ref, vat_ref, qb_ref, kb_ref, vb_ref, ycv_ref, hbuf_ref):
    i = pl.program_id(0)

    @pl.when(i == 0)
    def _():
        hbuf_ref[0:8, :] = jnp.zeros((8, CONV_CH), F32)

    tm = x_ref.shape[0] // IN_SPLIT
    for part in range(IN_SPLIT):
        _in_proj_rows(part * tm, tm, x_ref, nw_ref, w_ref, qnw_ref, knw_ref, cw_ref, onw_ref,
                      qa_ref, kat_ref, vat_ref, qb_ref, kb_ref, vb_ref, ycv_ref, hbuf_ref)


def _in_proj_rows(r0, tm, x_ref, nw_ref, w_ref, qnw_ref, knw_ref, cw_ref, onw_ref,
                  qa_ref, kat_ref, vat_ref, qb_ref, kb_ref, vb_ref, ycv_ref, hbuf_ref):
    rows = slice(r0, r0 + tm)
    xf = x_ref[rows, :]
    ms = jnp.mean(xf * xf, axis=-1, keepdims=True)
    h = ((xf * lax.rsqrt(ms + EPS)) * nw_ref[...]).astype(BF16)

    sb = _dot(h, w_ref[:, 0:3 * D_SB].astype(BF16))
    qa_ref[rows, :] = (sb[:, 0:D_SB] * (QK_SCALE * LOG2E)).astype(BF16)
    kat = sb[:, D_SB:2 * D_SB].T.astype(BF16)
    vat = sb[:, 2 * D_SB:3 * D_SB].T.astype(BF16)
    for t in range(tm // SB_T):
        kat_ref[r0 // SB_T + t] = kat[:, t * SB_T:(t + 1) * SB_T]
        vat_ref[r0 // SB_T + t] = vat[:, t * SB_T:(t + 1) * SB_T]

    rest = _dot(h, w_ref[:, 3 * D_SB:].astype(BF16))
    q_b = rest[:, 0:D_CH]
    k_b = rest[:, D_CH:2 * D_CH]
    qn = (q_b * lax.rsqrt(_group_mean_sq(q_b) + EPS)) * qnw_ref[...]
    qb_ref[rows, :] = (qn * (QK_SCALE * LOG2E)).astype(BF16)
    kn = (k_b * lax.rsqrt(_group_mean_sq(k_b) + EPS)) * knw_ref[...]
    kb_ref[rows, :] = kn.astype(BF16)
    vb_ref[rows, :] = rest[:, 2 * D_CH:3 * D_CH].astype(BF16)

    off = 3 * D_CH
    g_b = rest[:, off:off + CONV_CH]
    hc = rest[:, off + CONV_CH:off + 2 * CONV_CH] * rest[:, off + 2 * CONV_CH:off + 3 * CONV_CH]

    hbuf_ref[8:8 + tm, :] = hc
    h1 = hbuf_ref[7:7 + tm, :]
    h2 = hbuf_ref[6:6 + tm, :]
    cw = cw_ref[...]
    ycv = g_b * (cw[0:1, :] * h2 + cw[1:2, :] * h1 + cw[2:3, :] * hc)
    hbuf_ref[0:8, :] = hbuf_ref[tm:tm + 8, :]
    yn = (ycv * lax.rsqrt(_group_mean_sq(ycv) + EPS)) * onw_ref[...]
    ycv_ref[rows, :] = yn.astype(BF16)


def _in_proj(x, nw, w, qnw, knw, cw, onw_cv):
    s = x.shape[0]
    tm = TM_IN
    nk = s // SB_T
    const = lambda shape: pl.BlockSpec(shape, lambda i: (0,) * len(shape), pipeline_mode=pl.Buffered(1))
    return pl.pallas_call(
        _in_proj_kernel,
        grid=(s // tm,),
        in_specs=[
            pl.BlockSpec((tm, D_MODEL), lambda i: (i, 0)),
            const((1, D_MODEL)),
            const((D_MODEL, D_IN)),
            const((1, D_CH)),
            const((1, D_CH)),
            const((CONV_WIDTH, CONV_CH)),
            const((1, CONV_CH)),
        ],
        out_specs=[
            pl.BlockSpec((tm, D_SB), lambda i: (i, 0)),
            pl.BlockSpec((tm // SB_T, D_SB, SB_T), lambda i: (i, 0, 0)),
            pl.BlockSpec((tm // SB_T, D_SB, SB_T), lambda i: (i, 0, 0)),
            pl.BlockSpec((tm, D_CH), lambda i: (i, 0)),
            pl.BlockSpec((tm, D_CH), lambda i: (i, 0)),
            pl.BlockSpec((tm, D_CH), lambda i: (i, 0)),
            pl.BlockSpec((tm, CONV_CH), lambda i: (i, 0)),
        ],
        out_shape=[
            jax.ShapeDtypeStruct((s, D_SB), BF16),
            jax.ShapeDtypeStruct((nk, D_SB, SB_T), BF16),
            jax.ShapeDtypeStruct((nk, D_SB, SB_T), BF16),
            jax.ShapeDtypeStruct((s, D_CH), BF16),
            jax.ShapeDtypeStruct((s, D_CH), BF16),
            jax.ShapeDtypeStruct((s, D_CH), BF16),
            jax.ShapeDtypeStruct((s, CONV_CH), BF16),
        ],
        scratch_shapes=[pltpu.VMEM((tm // IN_SPLIT + 8, CONV_CH), F32)],
        compiler_params=pltpu.CompilerParams(
            dimension_semantics=("arbitrary",), vmem_limit_bytes=VMEM_LIMIT_BYTES),
        name="in_proj",
    )(x, nw, w, qnw, knw, cw, onw_cv)


def _sb_stage_a(j, qm_ref, kat_ref, buf, diag_off=None):
    ls_ref, sp_ref, sp0_ref = buf
    z = _dot(qm_ref[...], kat_ref[j])
    e = jnp.exp2(-jnp.abs(z))
    sp = jnp.maximum(z, 0.0) + jnp.log2(1.0 + e)
    ls = z - sp
    if diag_off is not None:
        t_idx = lax.broadcasted_iota(jnp.int32, z.shape, 0) % SB_TQ
        s_idx = lax.broadcasted_iota(jnp.int32, z.shape, 1) + diag_off
        mask = s_idx < t_idx
        sp = jnp.where(mask, sp, 0.0)
        ls = jnp.where(mask, ls, NEG_BIG)
    ls_ref[...] = ls
    sp_ref[...] = sp.astype(BF16)
    sp0_ref[...] = sp[:, 0:1]


def _sb_stage_b(j, buf, vat_ref, ntri, acc_ref, carry_ref, w_ref):
    ls_ref, sp_ref, sp0_ref = buf
    a = _dot(sp_ref[...], ntri) + carry_ref[...]
    w_ref[...] = jnp.exp2(ls_ref[...] + a).astype(BF16)
    carry_ref[...] = a[:, 0:1] - sp0_ref[...]
    vt = vat_ref[j]
    for h in range(N_HEADS_SB):
        acc_ref[h * HEAD_DIM:(h + 1) * HEAD_DIM, :] += _dot_nt(
            vt[h * HEAD_DIM:(h + 1) * HEAD_DIM, :], w_ref[h * SB_TQ:(h + 1) * SB_TQ, :])


def _sb_kernel(qa_ref, kat_ref, vat_ref, ntri_ref, onw_ref, o_ref, qm_ref, acc_ref, carry_ref,
               ls0_ref, sp0_ref, c0_ref, ls1_ref, sp1_ref, c1_ref, w_ref):
    i = pl.program_id(0)
    q = qa_ref[...]
    lane_head = lax.broadcasted_iota(jnp.int32, q.shape, 1) // HEAD_DIM
    for h in range(N_HEADS_SB):
        qm_ref[h * SB_TQ:(h + 1) * SB_TQ, :] = jnp.where(lane_head == h, q, jnp.zeros_like(q))
    acc_ref[...] = jnp.zeros_like(acc_ref)
    carry_ref[...] = jnp.zeros_like(carry_ref)
    ntri = ntri_ref[...]
    bufs = ((ls0_ref, sp0_ref, c0_ref), (ls1_ref, sp1_ref, c1_ref))
    stage_a = lambda j, b, off=None: _sb_stage_a(j, qm_ref, kat_ref, bufs[b], off)
    stage_b = lambda j, b: _sb_stage_b(j, bufs[b], vat_ref, ntri, acc_ref, carry_ref, w_ref)

    r = SB_TQ // SB_T
    stage_a(r * i + r - 1, 0, (r - 1) * SB_T)
    for d in range(1, r):
        stage_a(r * i + r - 1 - d, d % 2, (r - 1 - d) * SB_T)
        stage_b(r * i + r - d, (d - 1) % 2)
    b0 = (r - 1) % 2

    n_pairs = (r * i) // 2

    def pair(j):
        stage_a(j - 1, 1 - b0)
        stage_b(j, b0)
        stage_a(j - 2, b0)
        stage_b(j - 1, 1 - b0)

    @pl.when(n_pairs % 2 == 1)
    def _():
        pair(r * i)

    def body(p, c):
        j = r * i - 2 * (n_pairs % 2) - 4 * p
        pair(j)
        pair(j - 2)
        return c

    lax.fori_loop(0, n_pairs // 2, body, 0)
    stage_b(0, b0)

    acc = acc_ref[...]
    parts = []
    for h in range(N_HEADS_SB):
        a = acc[h * HEAD_DIM:(h + 1) * HEAD_DIM, :]
        ms = jnp.mean(a * a, axis=0, keepdims=True)
        parts.append(a * lax.rsqrt(ms + EPS))
    y = jnp.concatenate(parts, axis=0) * onw_ref[...]
    o_ref[...] = y.T.astype(BF16)


def _stickbreak(qa, kat3, vat3, ntri, onw_b):
    nk = kat3.shape[0]
    s = nk * SB_T
    rows = N_HEADS_SB * SB_TQ
    const = lambda shape: pl.BlockSpec(shape, lambda i: (0,) * len(shape), pipeline_mode=pl.Buffered(1))
    tile_bufs = [pltpu.VMEM((rows, SB_T), F32), pltpu.VMEM((rows, SB_T), BF16), pltpu.VMEM((rows, 1), F32)]
    return pl.pallas_call(
        _sb_kernel,
        grid=(s // SB_TQ,),
        in_specs=[
            pl.BlockSpec((SB_TQ, D_SB), lambda i: (i, 0)),
            const((nk, D_SB, SB_T)),
            const((nk, D_SB, SB_T)),
            const((SB_T, SB_T)),
            const((D_SB, SB_TQ)),
        ],
        out_specs=pl.BlockSpec((SB_TQ, D_SB), lambda i: (i, 0)),
        out_shape=jax.ShapeDtypeStruct((s, D_SB), BF16),
        scratch_shapes=[
            pltpu.VMEM((rows, D_SB), BF16),
            pltpu.VMEM((D_SB, SB_TQ), F32),
            pltpu.VMEM((rows, 1), F32),
            *tile_bufs, *tile_bufs,
            pltpu.VMEM((rows, SB_T), BF16),
        ],
        compiler_params=pltpu.CompilerParams(
            dimension_semantics=("arbitrary",), vmem_limit_bytes=VMEM_LIMIT_BYTES),
        name="stickbreak",
    )(qa, kat3, vat3, ntri, onw_b)


def _chunk_kernel(q_ref, kp_ref, kc_ref, vp_ref, vc_ref, gb_ref, onw_ref, o_ref, kbuf_ref, vbuf_ref, bias_ref):
    i = pl.program_id(0)
    tq = q_ref.shape[0]

    @pl.when(i == 0)
    def _():
        lane_w = lax.broadcasted_iota(jnp.int32, (CHUNK, CH_WIN), 1)
        for hd in range(N_HEADS_CH):
            base = jnp.broadcast_to(gb_ref[hd:hd + 1, :], (CHUNK, CH_WIN))
            first = pltpu.roll(base, CH_WIN - (CHUNK - 1), 1, stride=1, stride_axis=0)
            second = pltpu.roll(base, 1, 1, stride=1, stride_axis=0)
            bias_ref[hd, 0:CHUNK, :] = jnp.where(lane_w < BAND, first, NEG_BIG)
            bias_ref[hd, CHUNK:2 * CHUNK, :] = jnp.where(lane_w >= CHUNK, second, NEG_BIG)

    kbuf_ref[0:tq, :] = kp_ref[...]
    kbuf_ref[tq:2 * tq, :] = kc_ref[...]
    vbuf_ref[0:tq, :] = vp_ref[...]
    vbuf_ref[tq:2 * tq, :] = vc_ref[...]
    col = lax.broadcasted_iota(jnp.int32, (1, 2 * tq), 1)
    kmask = jnp.where(jnp.logical_and(col < tq, i == 0), NEG_BIG, 0.0).astype(F32)
    lane = lax.broadcasted_iota(jnp.int32, (CH_SUB, 2 * HEAD_DIM), 1)
    low = lane < HEAD_DIM
    n_pairs = N_HEADS_CH // 2
    for sub in range(tq // CH_SUB):
        r0 = sub * CH_SUB
        km = kmask[:, r0:r0 + CH_WIN]
        scores = []
        for p in range(n_pairs):
            c0 = p * 2 * HEAD_DIM
            qp = q_ref[r0:r0 + CH_SUB, c0:c0 + 2 * HEAD_DIM]
            qm = jnp.concatenate([jnp.where(low, qp, jnp.zeros_like(qp)),
                                  jnp.where(low, jnp.zeros_like(qp), qp)], axis=0)
            sc = _dot_nt(qm, kbuf_ref[r0:r0 + CH_WIN, c0:c0 + 2 * HEAD_DIM])
            bias = jnp.concatenate([bias_ref[2 * p], bias_ref[2 * p + 1]], axis=0)
            scores.append(sc + bias + km)
        probs, inv_l = [], []
        for p in range(n_pairs):
            m = jnp.max(scores[p], axis=-1, keepdims=True)
            pr = jnp.exp2(scores[p] - m)
            inv_l.append(1.0 / jnp.sum(pr, axis=-1, keepdims=True))
            probs.append(pr.astype(BF16))
        outs = []
        for p in range(n_pairs):
            c0 = p * 2 * HEAD_DIM
            o2 = _dot(probs[p], vbuf_ref[r0:r0 + CH_WIN, c0:c0 + 2 * HEAD_DIM]) * inv_l[p]
            outs.append(jnp.where(low, o2[0:CH_SUB], o2[CH_SUB:2 * CH_SUB]))
        for p in range(n_pairs):
            c0 = p * 2 * HEAD_DIM
            o = outs[p]
            y = (o * lax.rsqrt(_group_mean_sq(o) + EPS)) * onw_ref[:, c0:c0 + 2 * HEAD_DIM]
            o_ref[r0:r0 + CH_SUB, c0:c0 + 2 * HEAD_DIM] = y.astype(BF16)


def _chunkattn(qb, kb, vb, gb, onw_ch):
    s = qb.shape[0]
    tq = CH_TQ
    const = lambda shape: pl.BlockSpec(shape, lambda i: (0,) * len(shape), pipeline_mode=pl.Buffered(1))
    cur = pl.BlockSpec((tq, D_CH), lambda i: (i, 0))
    prev = pl.BlockSpec((tq, D_CH), lambda i: (jnp.maximum(i - 1, 0), 0))
    return pl.pallas_call(
        _chunk_kernel,
        grid=(s // tq,),
        in_specs=[cur, prev, cur, prev, cur,
                  const((N_HEADS_CH, CH_WIN)), const((1, D_CH))],
        out_specs=pl.BlockSpec((tq, D_CH), lambda i: (i, 0)),
        out_shape=jax.ShapeDtypeStruct((s, D_CH), BF16),
        scratch_shapes=[pltpu.VMEM((2 * tq, D_CH), BF16), pltpu.VMEM((2 * tq, D_CH), BF16),
                        pltpu.VMEM((N_HEADS_CH, CH_SUB, CH_WIN), F32)],
        compiler_params=pltpu.CompilerParams(
            dimension_semantics=("arbitrary",), vmem_limit_bytes=VMEM_LIMIT_BYTES),
        name="chunkattn",
    )(qb, kb, kb, vb, vb, gb, onw_ch)


def _out_ffn_kernel(x_ref, ysb_ref, ych_ref, ycv_ref, wo_ref, fnw_ref, wg_ref, wu_ref, wd_ref, o_ref):
    x1 = (x_ref[...]
          + _dot(ysb_ref[...], wo_ref[0:D_SB, :].astype(BF16))
          + _dot(ych_ref[...], wo_ref[D_SB:D_SB + D_CH, :].astype(BF16))
          + _dot(ycv_ref[...], wo_ref[D_SB + D_CH:, :].astype(BF16)))
    ms = jnp.mean(x1 * x1, axis=-1, keepdims=True)
    h = ((x1 * lax.rsqrt(ms + EPS)) * fnw_ref[...]).astype(BF16)
    gate = _dot(h, wg_ref[...])
    up = _dot(h, wu_ref[...])
    act = ((gate * (1.0 / (1.0 + jnp.exp(-gate)))) * up).astype(BF16)
    o_ref[...] = x1 + _dot(act, wd_ref[...])


def _out_ffn(x, ysb, ych, ycv, wo, fnw, wg, wu, wd):
    s = x.shape[0]
    tm = TM_OUT
    const = lambda shape: pl.BlockSpec(shape, lambda i: (0,) * len(shape), pipeline_mode=pl.Buffered(1))
    row = lambda w: pl.BlockSpec((tm, w), lambda i: (i, 0))
    return pl.pallas_call(
        _out_ffn_kernel,
        grid=(s // tm,),
        in_specs=[row(D_MODEL), row(D_SB), row(D_CH), row(CONV_CH),
                  const((D_MODEL, D_MODEL)), const((1, D_MODEL)),
                  const((D_MODEL, D_FF)), const((D_MODEL, D_FF)), const((D_FF, D_MODEL))],
        out_specs=row(D_MODEL),
        out_shape=jax.ShapeDtypeStruct((s, D_MODEL), F32),
        compiler_params=pltpu.CompilerParams(
            dimension_semantics=("arbitrary",), vmem_limit_bytes=VMEM_LIMIT_BYTES),
        name="out_ffn",
    )(x, ysb, ych, ycv, wo, fnw, wg, wu, wd)


def _bias_base_index():
    v = np.arange(CH_WIN)
    u = np.minimum(v - (CHUNK - 1), BAND - 1)
    return np.clip(N_PREV_CHUNKS * CHUNK - u, -REL_CLIP, REL_CLIP) + REL_CLIP


def kernel(x, attn_norm_w, w_in, q_norm_w, k_norm_w, rel_bias, conv_w, out_norm_w, w_out, ffn_norm_w,
           w_gate, w_up, w_down):
    b, s, d = x.shape
    assert b == 1 and d == D_MODEL and s % TM_IN == 0 and s % TM_OUT == 0 and s % CH_TQ == 0 and s % SB_TQ == 0
    depth = w_in.shape[0]
    xs = x.reshape(s, d)
    ar = np.arange(SB_T)
    ntri = jnp.asarray(-(ar[:, None] > ar[None, :]).astype(np.float32), dtype=BF16)
    for l in range(depth):
        onw = out_norm_w[l].astype(F32)
        qa, kat3, vat3, qb, kb, vb, ycv = _in_proj(
            xs, attn_norm_w[l].reshape(1, d), w_in[l],
            jnp.tile(q_norm_w[l], N_HEADS_CH).reshape(1, D_CH),
            jnp.tile(k_norm_w[l], N_HEADS_CH).reshape(1, D_CH),
            conv_w[l], onw[D_SB + D_CH:].reshape(1, CONV_CH))
        onw_sb = jnp.broadcast_to(onw[0:D_SB].reshape(D_SB, 1), (D_SB, SB_TQ))
        ysb = _stickbreak(qa, kat3, vat3, ntri, onw_sb)
        gb = rel_bias[l].astype(F32)[:, _bias_base_index()] * LOG2E
        ych = _chunkattn(qb, kb, vb, gb, onw[D_SB:D_SB + D_CH].reshape(1, D_CH))
        xs = _out_ffn(xs, ysb, ych, ycv, w_out[l], ffn_norm_w[l].reshape(1, d),
                      w_gate[l].astype(BF16), w_up[l].astype(BF16), w_down[l].astype(BF16))
    return xs.reshape(b, s, d)
```

```python
import jax
import jax.numpy as jnp
import numpy as np
from jax import lax
from jax.experimental import pallas as pl
from jax.experimental.pallas import tpu as pltpu

F32 = jnp.float32
BF16 = jnp.bfloat16

D_MODEL = 1024
HEAD_DIM = 64
N_HEADS_SB = 4
N_HEADS_CH = 8
CONV_CH = 256
CONV_WIDTH = 3
CHUNK = 64
N_PREV_CHUNKS = 8
BAND = (N_PREV_CHUNKS + 1) * CHUNK
REL_CLIP = 128
D_SB = N_HEADS_SB * HEAD_DIM
D_CH = N_HEADS_CH * HEAD_DIM
D_IN = 3 * D_SB + 3 * D_CH + 3 * CONV_CH
D_FF = 2816
EPS = 1e-6

LOG2E = 1.4426950408889634
QK_SCALE = HEAD_DIM ** -0.5
NEG_BIG = -1e30

VMEM_LIMIT_BYTES = 56 * 1024 * 1024

TM_IN = 1024
IN_SPLIT = 2
TM_OUT = 512
SB_T = 256
SB_TQ = 512
CH_TQ = 512
CH_SUB = 128
CH_WIN = CH_SUB + N_PREV_CHUNKS * CHUNK


def _layer_spec(layer, shape):
    return pl.BlockSpec((None,) + shape, lambda i: (layer,) + (0,) * len(shape), pipeline_mode=pl.Buffered(1))


def _dot(a, b):
    return jnp.dot(a, b, preferred_element_type=F32)


def _dot_nt(a, b):
    return lax.dot_general(a, b, (((1,), (1,)), ((), ())), preferred_element_type=F32)


def _group_mean_sq(x):
    low = lax.broadcasted_iota(jnp.int32, (x.shape[0], 2 * HEAD_DIM), 1) < HEAD_DIM
    parts = []
    for c in range(0, x.shape[1], 2 * HEAD_DIM):
        sq = x[:, c:c + 2 * HEAD_DIM]
        sq = sq * sq
        lo = jnp.sum(jnp.where(low, sq, 0.0), axis=-1, keepdims=True)
        hi = jnp.sum(jnp.where(low, 0.0, sq), axis=-1, keepdims=True)
        parts.append(jnp.where(low, lo, hi) * (1.0 / HEAD_DIM))
    return parts[0] if len(parts) == 1 else jnp.concatenate(parts, axis=1)


def _in_proj_kernel(x_ref, nw_ref, w_ref, qnw_ref, knw_ref, cw_ref, onw_ref,
                    qa_ref, kat_ref, vat_ref, qb_ref, kb_ref, vb_ref, ycv_ref, hbuf_ref):
    i = pl.program_id(0)

    @pl.when(i == 0)
    def _():
        hbuf_ref[0:8, :] = jnp.zeros((8, CONV_CH), F32)

    tm = x_ref.shape[0] // IN_SPLIT
    for part in range(IN_SPLIT):
        _in_proj_rows(part * tm, tm, x_ref, nw_ref, w_ref, qnw_ref, knw_ref, cw_ref, onw_ref,
                      qa_ref, kat_ref, vat_ref, qb_ref, kb_ref, vb_ref, ycv_ref, hbuf_ref)


def _in_proj_rows(r0, tm, x_ref, nw_ref, w_ref, qnw_ref, knw_ref, cw_ref, onw_ref,
                  qa_ref, kat_ref, vat_ref, qb_ref, kb_ref, vb_ref, ycv_ref, hbuf_ref):
    rows = slice(r0, r0 + tm)
    xf = x_ref[rows, :]
    ms = jnp.mean(xf * xf, axis=-1, keepdims=True)
    h = ((xf * lax.rsqrt(ms + EPS)) * nw_ref[...]).astype(BF16)

    sb = _dot(h, w_ref[:, 0:3 * D_SB].astype(BF16))
    qa_ref[rows, :] = (sb[:, 0:D_SB] * (QK_SCALE * LOG2E)).astype(BF16)
    kat = sb[:, D_SB:2 * D_SB].T.astype(BF16)
    vat = sb[:, 2 * D_SB:3 * D_SB].T.astype(BF16)
    for t in range(tm // SB_T):
        kat_ref[r0 // SB_T + t] = kat[:, t * SB_T:(t + 1) * SB_T]
        vat_ref[r0 // SB_T + t] = vat[:, t * SB_T:(t + 1) * SB_T]

    rest = _dot(h, w_ref[:, 3 * D_SB:].astype(BF16))
    q_b = rest[:, 0:D_CH]
    k_b = rest[:, D_CH:2 * D_CH]
    qn = (q_b * lax.rsqrt(_group_mean_sq(q_b) + EPS)) * qnw_ref[...]
    qb_ref[rows, :] = (qn * (QK_SCALE * LOG2E)).astype(BF16)
    kn = (k_b * lax.rsqrt(_group_mean_sq(k_b) + EPS)) * knw_ref[...]
    kb_ref[rows, :] = kn.astype(BF16)
    vb_ref[rows, :] = rest[:, 2 * D_CH:3 * D_CH].astype(BF16)

    off = 3 * D_CH
    g_b = rest[:, off:off + CONV_CH]
    hc = rest[:, off + CONV_CH:off + 2 * CONV_CH] * rest[:, off + 2 * CONV_CH:off + 3 * CONV_CH]

    hbuf_ref[8:8 + tm, :] = hc
    h1 = hbuf_ref[7:7 + tm, :]
    h2 = hbuf_ref[6:6 + tm, :]
    cw = cw_ref[...]
    ycv = g_b * (cw[0:1, :] * h2 + cw[1:2, :] * h1 + cw[2:3, :] * hc)
    hbuf_ref[0:8, :] = hbuf_ref[tm:tm + 8, :]
    yn = (ycv * lax.rsqrt(_group_mean_sq(ycv) + EPS)) * onw_ref[...]
    ycv_ref[rows, :] = yn.astype(BF16)


def _in_proj(x, nw, w_all, layer, qnw, knw, cw, onw_cv):
    s = x.shape[0]
    tm = TM_IN
    nk = s // SB_T
    const = lambda shape: pl.BlockSpec(shape, lambda i: (0,) * len(shape), pipeline_mode=pl.Buffered(1))
    return pl.pallas_call(
        _in_proj_kernel,
        grid=(s // tm,),
        in_specs=[
            pl.BlockSpec((tm, D_MODEL), lambda i: (i, 0)),
            const((1, D_MODEL)),
            _layer_spec(layer, (D_MODEL, D_IN)),
            const((1, D_CH)),
            const((1, D_CH)),
            const((CONV_WIDTH, CONV_CH)),
            const((1, CONV_CH)),
        ],
        out_specs=[
            pl.BlockSpec((tm, D_SB), lambda i: (i, 0)),
            pl.BlockSpec((tm // SB_T, D_SB, SB_T), lambda i: (i, 0, 0)),
            pl.BlockSpec((tm // SB_T, D_SB, SB_T), lambda i: (i, 0, 0)),
            pl.BlockSpec((tm, D_CH), lambda i: (i, 0)),
            pl.BlockSpec((tm, D_CH), lambda i: (i, 0)),
            pl.BlockSpec((tm, D_CH), lambda i: (i, 0)),
            pl.BlockSpec((tm, CONV_CH), lambda i: (i, 0)),
        ],
        out_shape=[
            jax.ShapeDtypeStruct((s, D_SB), BF16),
            jax.ShapeDtypeStruct((nk, D_SB, SB_T), BF16),
            jax.ShapeDtypeStruct((nk, D_SB, SB_T), BF16),
            jax.ShapeDtypeStruct((s, D_CH), BF16),
            jax.ShapeDtypeStruct((s, D_CH), BF16),
            jax.ShapeDtypeStruct((s, D_CH), BF16),
            jax.ShapeDtypeStruct((s, CONV_CH), BF16),
        ],
        scratch_shapes=[pltpu.VMEM((tm // IN_SPLIT + 8, CONV_CH), F32)],
        compiler_params=pltpu.CompilerParams(
            dimension_semantics=("arbitrary",), vmem_limit_bytes=VMEM_LIMIT_BYTES),
        name="in_proj",
    )(x, nw, w_all, qnw, knw, cw, onw_cv)


def _sb_stage_a(j, qm_ref, kat_ref, buf, diag_off=None):
    ls_ref, sp_ref, sp0_ref = buf
    z = _dot(qm_ref[...], kat_ref[j])
    e = jnp.exp2(-jnp.abs(z))
    sp = jnp.maximum(z, 0.0) + jnp.log2(1.0 + e)
    ls = z - sp
    if diag_off is not None:
        t_idx = lax.broadcasted_iota(jnp.int32, z.shape, 0) % SB_TQ
        s_idx = lax.broadcasted_iota(jnp.int32, z.shape, 1) + diag_off
        mask = s_idx < t_idx
        sp = jnp.where(mask, sp, 0.0)
        ls = jnp.where(mask, ls, NEG_BIG)
    ls_ref[...] = ls
    sp_ref[...] = sp.astype(BF16)
    sp0_ref[...] = sp[:, 0:1]


def _sb_stage_b(j, buf, vat_ref, ntri, acc_ref, carry_ref, w_ref):
    ls_ref, sp_ref, sp0_ref = buf
    a = _dot(sp_ref[...], ntri) + carry_ref[...]
    w_ref[...] = jnp.exp2(ls_ref[...] + a).astype(BF16)
    carry_ref[...] = a[:, 0:1] - sp0_ref[...]
    vt = vat_ref[j]
    for h in range(N_HEADS_SB):
        acc_ref[h * HEAD_DIM:(h + 1) * HEAD_DIM, :] += _dot_nt(
            vt[h * HEAD_DIM:(h + 1) * HEAD_DIM, :], w_ref[h * SB_TQ:(h + 1) * SB_TQ, :])


def _sb_kernel(qa_ref, kat_ref, vat_ref, ntri_ref, onw_ref, o_ref, qm_ref, acc_ref, carry_ref,
               ls0_ref, sp0_ref, c0_ref, ls1_ref, sp1_ref, c1_ref, w_ref):
    i = pl.program_id(0)
    q = qa_ref[...]
    lane_head = lax.broadcasted_iota(jnp.int32, q.shape, 1) // HEAD_DIM
    for h in range(N_HEADS_SB):
        qm_ref[h * SB_TQ:(h + 1) * SB_TQ, :] = jnp.where(lane_head == h, q, jnp.zeros_like(q))
    acc_ref[...] = jnp.zeros_like(acc_ref)
    carry_ref[...] = jnp.zeros_like(carry_ref)
    ntri = ntri_ref[...]
    bufs = ((ls0_ref, sp0_ref, c0_ref), (ls1_ref, sp1_ref, c1_ref))
    stage_a = lambda j, b, off=None: _sb_stage_a(j, qm_ref, kat_ref, bufs[b], off)
    stage_b = lambda j, b: _sb_stage_b(j, bufs[b], vat_ref, ntri, acc_ref, carry_ref, w_ref)

    r = SB_TQ // SB_T
    stage_a(r * i + r - 1, 0, (r - 1) * SB_T)
    for d in range(1, r):
        stage_a(r * i + r - 1 - d, d % 2, (r - 1 - d) * SB_T)
        stage_b(r * i + r - d, (d - 1) % 2)
    b0 = (r - 1) % 2

    n_pairs = (r * i) // 2

    def pair(j):
        stage_a(j - 1, 1 - b0)
        stage_b(j, b0)
        stage_a(j - 2, b0)
        stage_b(j - 1, 1 - b0)

    @pl.when(n_pairs % 2 == 1)
    def _():
        pair(r * i)

    def body(p, c):
        j = r * i - 2 * (n_pairs % 2) - 4 * p
        pair(j)
        pair(j - 2)
        return c

    lax.fori_loop(0, n_pairs // 2, body, 0)
    stage_b(0, b0)

    acc = acc_ref[...]
    parts = []
    for h in range(N_HEADS_SB):
        a = acc[h * HEAD_DIM:(h + 1) * HEAD_DIM, :]
        ms = jnp.mean(a * a, axis=0, keepdims=True)
        parts.append(a * lax.rsqrt(ms + EPS))
    y = jnp.concatenate(parts, axis=0) * onw_ref[...]
    o_ref[...] = y.T.astype(BF16)


def _stickbreak(qa, kat3, vat3, ntri, onw_b):
    nk = kat3.shape[0]
    s = nk * SB_T
    rows = N_HEADS_SB * SB_TQ
    const = lambda shape: pl.BlockSpec(shape, lambda i: (0,) * len(shape), pipeline_mode=pl.Buffered(1))
    tile_bufs = [pltpu.VMEM((rows, SB_T), F32), pltpu.VMEM((rows, SB_T), BF16), pltpu.VMEM((rows, 1), F32)]
    return pl.pallas_call(
        _sb_kernel,
        grid=(s // SB_TQ,),
        in_specs=[
            pl.BlockSpec((SB_TQ, D_SB), lambda i: (i, 0)),
            const((nk, D_SB, SB_T)),
            const((nk, D_SB, SB_T)),
            const((SB_T, SB_T)),
            const((D_SB, SB_TQ)),
        ],
        out_specs=pl.BlockSpec((SB_TQ, D_SB), lambda i: (i, 0)),
        out_shape=jax.ShapeDtypeStruct((s, D_SB), BF16),
        scratch_shapes=[
            pltpu.VMEM((rows, D_SB), BF16),
            pltpu.VMEM((D_SB, SB_TQ), F32),
            pltpu.VMEM((rows, 1), F32),
            *tile_bufs, *tile_bufs,
            pltpu.VMEM((rows, SB_T), BF16),
        ],
        compiler_params=pltpu.CompilerParams(
            dimension_semantics=("arbitrary",), vmem_limit_bytes=VMEM_LIMIT_BYTES),
        name="stickbreak",
    )(qa, kat3, vat3, ntri, onw_b)


def _chunk_kernel(q_ref, kp_ref, kc_ref, vp_ref, vc_ref, gb_ref, onw_ref, o_ref, kbuf_ref, vbuf_ref, bias_ref):
    i = pl.program_id(0)
    tq = q_ref.shape[0]

    @pl.when(i == 0)
    def _():
        lane_w = lax.broadcasted_iota(jnp.int32, (CHUNK, CH_WIN), 1)
        for hd in range(N_HEADS_CH):
            base = jnp.broadcast_to(gb_ref[hd:hd + 1, :], (CHUNK, CH_WIN))
            first = pltpu.roll(base, CH_WIN - (CHUNK - 1), 1, stride=1, stride_axis=0)
            second = pltpu.roll(base, 1, 1, stride=1, stride_axis=0)
            bias_ref[hd, 0:CHUNK, :] = jnp.where(lane_w < BAND, first, NEG_BIG)
            bias_ref[hd, CHUNK:2 * CHUNK, :] = jnp.where(lane_w >= CHUNK, second, NEG_BIG)

    kbuf_ref[0:tq, :] = kp_ref[...]
    kbuf_ref[tq:2 * tq, :] = kc_ref[...]
    vbuf_ref[0:tq, :] = vp_ref[...]
    vbuf_ref[tq:2 * tq, :] = vc_ref[...]
    col = lax.broadcasted_iota(jnp.int32, (1, 2 * tq), 1)
    kmask = jnp.where(jnp.logical_and(col < tq, i == 0), NEG_BIG, 0.0).astype(F32)
    lane = lax.broadcasted_iota(jnp.int32, (CH_SUB, 2 * HEAD_DIM), 1)
    low = lane < HEAD_DIM
    n_pairs = N_HEADS_CH // 2
    for sub in range(tq // CH_SUB):
        r0 = sub * CH_SUB
        km = kmask[:, r0:r0 + CH_WIN]
        scores = []
        for p in range(n_pairs):
            c0 = p * 2 * HEAD_DIM
            qp = q_ref[r0:r0 + CH_SUB, c0:c0 + 2 * HEAD_DIM]
            qm = jnp.concatenate([jnp.where(low, qp, jnp.zeros_like(qp)),
                                  jnp.where(low, jnp.zeros_like(qp), qp)], axis=0)
            sc = _dot_nt(qm, kbuf_ref[r0:r0 + CH_WIN, c0:c0 + 2 * HEAD_DIM])
            bias = jnp.concatenate([bias_ref[2 * p], bias_ref[2 * p + 1]], axis=0)
            scores.append(sc + bias + km)
        probs, inv_l = [], []
        for p in range(n_pairs):
            m = jnp.max(scores[p], axis=-1, keepdims=True)
            pr = jnp.exp2(scores[p] - m)
            inv_l.append(1.0 / jnp.sum(pr, axis=-1, keepdims=True))
            probs.append(pr.astype(BF16))
        outs = []
        for p in range(n_pairs):
            c0 = p * 2 * HEAD_DIM
            o2 = _dot(probs[p], vbuf_ref[r0:r0 + CH_WIN, c0:c0 + 2 * HEAD_DIM]) * inv_l[p]
            outs.append(jnp.where(low, o2[0:CH_SUB], o2[CH_SUB:2 * CH_SUB]))
        for p in range(n_pairs):
            c0 = p * 2 * HEAD_DIM
            o = outs[p]
            y = (o * lax.rsqrt(_group_mean_sq(o) + EPS)) * onw_ref[:, c0:c0 + 2 * HEAD_DIM]
            o_ref[r0:r0 + CH_SUB, c0:c0 + 2 * HEAD_DIM] = y.astype(BF16)


def _chunkattn(qb, kb, vb, gb, onw_ch):
    s = qb.shape[0]
    tq = CH_TQ
    const = lambda shape: pl.BlockSpec(shape, lambda i: (0,) * len(shape), pipeline_mode=pl.Buffered(1))
    cur = pl.BlockSpec((tq, D_CH), lambda i: (i, 0))
    prev = pl.BlockSpec((tq, D_CH), lambda i: (jnp.maximum(i - 1, 0), 0))
    return pl.pallas_call(
        _chunk_kernel,
        grid=(s // tq,),
        in_specs=[cur, prev, cur, prev, cur,
                  const((N_HEADS_CH, CH_WIN)), const((1, D_CH))],
        out_specs=pl.BlockSpec((tq, D_CH), lambda i: (i, 0)),
        out_shape=jax.ShapeDtypeStruct((s, D_CH), BF16),
        scratch_shapes=[pltpu.VMEM((2 * tq, D_CH), BF16), pltpu.VMEM((2 * tq, D_CH), BF16),
                        pltpu.VMEM((N_HEADS_CH, CH_SUB, CH_WIN), F32)],
        compiler_params=pltpu.CompilerParams(
            dimension_semantics=("arbitrary",), vmem_limit_bytes=VMEM_LIMIT_BYTES),
        name="chunkattn",
    )(qb, kb, kb, vb, vb, gb, onw_ch)


def _out_ffn_kernel(x_ref, ysb_ref, ych_ref, ycv_ref, wo_ref, fnw_ref, wg_ref, wu_ref, wd_ref, o_ref):
    x1 = (x_ref[...]
          + _dot(ysb_ref[...], wo_ref[0:D_SB, :].astype(BF16))
          + _dot(ych_ref[...], wo_ref[D_SB:D_SB + D_CH, :].astype(BF16))
          + _dot(ycv_ref[...], wo_ref[D_SB + D_CH:, :].astype(BF16)))
    ms = jnp.mean(x1 * x1, axis=-1, keepdims=True)
    h = ((x1 * lax.rsqrt(ms + EPS)) * fnw_ref[...]).astype(BF16)
    gate = _dot(h, wg_ref[...])
    up = _dot(h, wu_ref[...])
    act = ((gate * (1.0 / (1.0 + jnp.exp(-gate)))) * up).astype(BF16)
    o_ref[...] = x1 + _dot(act, wd_ref[...])


def _out_ffn(x, ysb, ych, ycv, layer, wo, fnw, wg, wu, wd):
    s = x.shape[0]
    tm = TM_OUT
    const = lambda shape: pl.BlockSpec(shape, lambda i: (0,) * len(shape), pipeline_mode=pl.Buffered(1))
    row = lambda w: pl.BlockSpec((tm, w), lambda i: (i, 0))
    return pl.pallas_call(
        _out_ffn_kernel,
        grid=(s // tm,),
        in_specs=[row(D_MODEL), row(D_SB), row(D_CH), row(CONV_CH),
                  _layer_spec(layer, (D_MODEL, D_MODEL)), const((1, D_MODEL)),
                  _layer_spec(layer, (D_MODEL, D_FF)), _layer_spec(layer, (D_MODEL, D_FF)),
                  _layer_spec(layer, (D_FF, D_MODEL))],
        out_specs=row(D_MODEL),
        out_shape=jax.ShapeDtypeStruct((s, D_MODEL), F32),
        compiler_params=pltpu.CompilerParams(
            dimension_semantics=("arbitrary",), vmem_limit_bytes=VMEM_LIMIT_BYTES),
        name="out_ffn",
    )(x, ysb, ych, ycv, wo, fnw, wg, wu, wd)


def _bias_base_index():
    v = np.arange(CH_WIN)
    u = np.minimum(v - (CHUNK - 1), BAND - 1)
    return np.clip(N_PREV_CHUNKS * CHUNK - u, -REL_CLIP, REL_CLIP) + REL_CLIP


def kernel(x, attn_norm_w, w_in, q_norm_w, k_norm_w, rel_bias, conv_w, out_norm_w, w_out, ffn_norm_w,
           w_gate, w_up, w_down):
    b, s, d = x.shape
    assert b == 1 and d == D_MODEL and s % TM_IN == 0 and s % TM_OUT == 0 and s % CH_TQ == 0 and s % SB_TQ == 0
    depth = w_in.shape[0]
    xs = x.reshape(s, d)
    ar = np.arange(SB_T)
    ntri = jnp.asarray(-(ar[:, None] > ar[None, :]).astype(np.float32), dtype=BF16)
    wg_all, wu_all, wd_all = w_gate.astype(BF16), w_up.astype(BF16), w_down.astype(BF16)
    for l in range(depth):
        onw = out_norm_w[l].astype(F32)
        qa, kat3, vat3, qb, kb, vb, ycv = _in_proj(
            xs, attn_norm_w[l].reshape(1, d), w_in, l,
            jnp.tile(q_norm_w[l], N_HEADS_CH).reshape(1, D_CH),
            jnp.tile(k_norm_w[l], N_HEADS_CH).reshape(1, D_CH),
            conv_w[l], onw[D_SB + D_CH:].reshape(1, CONV_CH))
        onw_sb = jnp.broadcast_to(onw[0:D_SB].reshape(D_SB, 1), (D_SB, SB_TQ))
        ysb = _stickbreak(qa, kat3, vat3, ntri, onw_sb)
        gb = rel_bias[l].astype(F32)[:, _bias_base_index()] * LOG2E
        ych = _chunkattn(qb, kb, vb, gb, onw[D_SB:D_SB + D_CH].reshape(1, D_CH))
        xs = _out_ffn(xs, ysb, ych, ycv, l, w_out, ffn_norm_w[l].reshape(1, d), wg_all, wu_all, wd_all)
    return xs.reshape(b, s, d)
```
